```python
import math
import jax, jax.numpy as jnp
from jax import lax
import numpy as np

D_MODEL = 1024
BATCH = 2
SEQ = 8192
DEPTH = 2

N_MEM = 256
EPS = 1e-6
MLA_HEADS = 8
MLA_Q_RANK = 384
MLA_KV_RANK = 256
MLA_NOPE = 64
MLA_ROPE = 32
MLA_V = 64
ROPE_BASE = 10000.0
Q_BLOCK = 128
HG_HEADS = 4
HG_KEY = 128
HG_VAL = 128
HG_CHUNK = 64
HG_KW = HG_HEADS * HG_KEY
HG_VW = HG_HEADS * HG_VAL
GM_GROUPS = 4
GM_GROUP_CH = 128
GM_CHUNK = 128
GM_WIDTH = GM_GROUPS * GM_GROUP_CH
X_HEADS = 4
X_HEAD_DIM = D_MODEL // X_HEADS
D_FF = -(-8 * D_MODEL // (3 * 256)) * 256
IN_SIZES = (MLA_Q_RANK, MLA_KV_RANK, MLA_ROPE,
            HG_KW, HG_KW, HG_VW, HG_VW,
            2 * GM_WIDTH,
            3 * D_MODEL)
IN_WIDTH = sum(IN_SIZES)

kernel_name = "hybrid_mla_hgrn2_gmlp_block"


def rms_norm(x, g):
    xf = x.astype(jnp.float32)
    y = xf * lax.rsqrt(jnp.mean(xf * xf, axis=-1, keepdims=True) + EPS)
    return (y * g.astype(jnp.float32)).astype(x.dtype)


def layer_norm(x, g, b):
    xf = x.astype(jnp.float32)
    mu = jnp.mean(xf, axis=-1, keepdims=True)
    var = jnp.mean(jnp.square(xf - mu), axis=-1, keepdims=True)
    y = (xf - mu) * lax.rsqrt(var + EPS)
    return (y * g.astype(jnp.float32) + b.astype(jnp.float32)).astype(x.dtype)


def split_sizes(z, sizes):
    out, start = [], 0
    for s in sizes:
        out.append(z[..., start:start + s])
        start += s
    return out


def apply_rope(x, cos, sin):
    xf = x.astype(jnp.float32)
    x1, x2 = jnp.split(xf, 2, axis=-1)
    y = jnp.concatenate([x1 * cos - x2 * sin, x1 * sin + x2 * cos], axis=-1)
    return y.astype(x.dtype)


def mla(c_q, c_kv, k_r, q_norm, w_uq, kv_norm, w_ukv, cos, sin):
    B, S, _ = c_q.shape
    q = (rms_norm(c_q, q_norm) @ w_uq).reshape(B, S, MLA_HEADS, MLA_NOPE + MLA_ROPE)
    q_nope, q_rope = q[..., :MLA_NOPE], q[..., MLA_NOPE:]
    q_rope = apply_rope(q_rope, cos[:, :, None, :], sin[:, :, None, :])
    kv = (rms_norm(c_kv, kv_norm) @ w_ukv).reshape(B, S, MLA_HEADS, MLA_NOPE + MLA_V)
    k_nope, v = kv[..., :MLA_NOPE], kv[..., MLA_NOPE:]
    k_rope = apply_rope(k_r, cos, sin)
    q = jnp.concatenate([q_nope, q_rope], axis=-1)
    k = jnp.concatenate([k_nope, jnp.broadcast_to(k_rope[:, :, None, :], (B, S, MLA_HEADS, MLA_ROPE))], axis=-1)
    scale = (MLA_NOPE + MLA_ROPE) ** -0.5
    nb = S // Q_BLOCK
    q_blocks = q.reshape(B, nb, Q_BLOCK, MLA_HEADS, MLA_NOPE + MLA_ROPE).transpose(1, 0, 2, 3, 4)
    key_idx = jnp.arange(S)

    def block(args):
        qb, bi = args
        s = jnp.einsum('bqhd,bkhd->bhqk', qb, k).astype(jnp.float32) * scale
        q_idx = bi * Q_BLOCK + jnp.arange(Q_BLOCK)
        mask = key_idx[None, :] <= q_idx[:, None]
        s = jnp.where(mask[None, None], s, jnp.finfo(jnp.float32).min)
        p = jax.nn.softmax(s, axis=-1).astype(v.dtype)
        return jnp.einsum('bhqk,bkhd->bqhd', p, v)

    o = lax.map(block, (q_blocks, jnp.arange(nb)))
    return o.transpose(1, 0, 2, 3, 4).reshape(B, S, MLA_HEADS * MLA_V)


def hgrn2(q_in, f_in, i_in, g_in, lb, head_norm):
    B, S, _ = q_in.shape
    dt = q_in.dtype
    f32 = jnp.float32
    lb = lb.reshape(HG_HEADS, HG_KEY).astype(f32)
    fr = f_in.reshape(B, S, HG_HEADS, HG_KEY).astype(f32)
    log_f = jnp.logaddexp(jnp.log(lb), jnp.log1p(-lb) + jax.nn.log_sigmoid(fr))
    k = (1.0 - lb) * jax.nn.sigmoid(-fr)
    q = jax.nn.silu(q_in.astype(f32)).reshape(B, S, HG_HEADS, HG_KEY)
    v = i_in.astype(f32).reshape(B, S, HG_HEADS, HG_VAL)
    nc = S // HG_CHUNK

    def to_chunks(t, d):
        return t.reshape(B, nc, HG_CHUNK, HG_HEADS, d).transpose(1, 0, 3, 2, 4)

    causal = jnp.tril(jnp.ones((HG_CHUNK, HG_CHUNK), dtype=bool))

    def step(state, xs):
        qc, kc, vc, lfc = xs
        b = jnp.cumsum(lfc, axis=2)
        o_inter = jnp.einsum('bhck,bhkv->bhcv', qc * jnp.exp(b), state)
        diff = b[:, :, :, None, :] - b[:, :, None, :, :]
        decay = jnp.exp(jnp.where(causal[None, None, :, :, None], diff, -jnp.inf))
        attn = jnp.einsum('bhtk,bhtsk,bhsk->bhts', qc, decay, kc)
        o = o_inter + jnp.einsum('bhts,bhsv->bhtv', attn, vc)
        b_last = b[:, :, -1:, :]
        new_state = jnp.exp(b_last[:, :, 0, :])[..., None] * state + \
            jnp.einsum('bhsk,bhsv->bhkv', kc * jnp.exp(b_last - b), vc)
        return new_state, o

    state0 = jnp.zeros((B, HG_HEADS, HG_KEY, HG_VAL), f32)
    _, o = lax.scan(step, state0, (to_chunks(q, HG_KEY), to_chunks(k, HG_KEY),
                                   to_chunks(v, HG_VAL), to_chunks(log_f, HG_KEY)))
    o = o.transpose(1, 0, 3, 2, 4).reshape(B, S, HG_HEADS, HG_VAL)
    g = g_in.astype(f32).reshape(B, S, HG_HEADS, HG_VAL)
    o = rms_norm(o, head_norm) * jax.nn.silu(g)
    return o.reshape(B, S, HG_VW).astype(dt)


def gmlp(uv_in, ln_g, ln_b, w_s, b_s):
    B, S, _ = uv_in.shape
    u, v = jnp.split(jax.nn.gelu(uv_in, approximate=False), 2, axis=-1)
    v = layer_norm(v, ln_g, ln_b)
    nc = S // GM_CHUNK
    v = v.reshape(B, nc, GM_CHUNK, GM_GROUPS, GM_GROUP_CH)
    w = w_s * jnp.tril(jnp.ones((GM_CHUNK, GM_CHUNK), w_s.dtype))[None]
    mixed = jnp.einsum('gts,bnsgc->bntgc', w, v) + b_s.T[None, None, :, :, None]
    return u * mixed.reshape(B, S, GM_WIDTH)


def cross_attn(h, m, w_q, w_kv, w_o):
    B, S, _ = h.shape
    q = (h @ w_q).reshape(B, S, X_HEADS, X_HEAD_DIM)
    kv = (m @ w_kv).reshape(B, m.shape[1], 2, X_HEADS, X_HEAD_DIM)
    k, v = kv[:, :, 0], kv[:, :, 1]
    s = jnp.einsum('bqhd,bkhd->bhqk', q, k).astype(jnp.float32) * (X_HEAD_DIM ** -0.5)
    p = jax.nn.softmax(s, axis=-1).astype(v.dtype)
    o = jnp.einsum('bhqk,bkhd->bqhd', p, v).reshape(B, S, D_MODEL)
    return o @ w_o


def setup_inputs(seed: int = 0) -> dict:
    key = jax.random.key(seed)
    ks = iter(jax.random.split(key, 40))
    f32 = jnp.float32

    def w(shape, fan_in):
        return jax.random.normal(next(ks), shape, f32) * fan_in ** -0.5

    def gain(shape):
        return 1.0 + 0.05 * jax.random.normal(next(ks), shape, f32)

    L = DEPTH
    x = jax.random.normal(next(ks), (BATCH, SEQ, D_MODEL), f32)
    mem = jax.random.normal(next(ks), (BATCH, N_MEM, D_MODEL), f32)
    offset = jax.random.randint(next(ks), (BATCH, 1), 0, 1024, dtype=jnp.int32)
    positions = offset + jnp.arange(SEQ, dtype=jnp.int32)[None, :]
    return {
        "x": x, "mem": mem, "positions": positions,
        "mix_norm": gain((L, D_MODEL)),
        "w_in": w((L, D_MODEL, IN_WIDTH), D_MODEL),
        "mla_q_norm": gain((L, MLA_Q_RANK)),
        "mla_w_uq": w((L, MLA_Q_RANK, MLA_HEADS * (MLA_NOPE + MLA_ROPE)), MLA_Q_RANK),
        "mla_kv_norm": gain((L, MLA_KV_RANK)),
        "mla_w_ukv": w((L, MLA_KV_RANK, MLA_HEADS * (MLA_NOPE + MLA_V)), MLA_KV_RANK),
        "hg_lower_bounds": 0.5 * jax.random.normal(next(ks), (L, HG_KW), f32),
        "hg_head_norm": gain((L, HG_VAL)),
        "gm_ln_g": gain((L, GM_WIDTH)),
        "gm_ln_b": 0.02 * jax.random.normal(next(ks), (L, GM_WIDTH), f32),
        "gm_w_s": w((L, GM_GROUPS, GM_CHUNK, GM_CHUNK), GM_CHUNK),
        "gm_b_s": 1.0 + 0.02 * jax.random.normal(next(ks), (L, GM_GROUPS, GM_CHUNK), f32),
        "w_branch_a": w((L, MLA_HEADS * MLA_V, D_MODEL), MLA_HEADS * MLA_V),
        "w_branch_b": w((L, HG_VW, D_MODEL), HG_VW),
        "w_branch_c": w((L, GM_WIDTH, D_MODEL), GM_WIDTH),
        "w_mix_out": w((L, D_MODEL, D_MODEL), D_MODEL),
        "xa_norm": gain((L, D_MODEL)),
        "mem_norm": gain((L, D_MODEL)),
        "xa_w_q": w((L, D_MODEL, D_MODEL), D_MODEL),
        "xa_w_kv": w((L, D_MODEL, 2 * D_MODEL), D_MODEL),
        "xa_w_o": w((L, D_MODEL, D_MODEL), D_MODEL),
        "ffn_norm": gain((L, D_MODEL)),
        "ffn_w_in": w((L, D_MODEL, 2 * D_FF), D_MODEL),
        "ffn_w_out": w((L, D_FF, D_MODEL), D_FF),
        "final_norm": gain((D_MODEL,)),
    }


def reference(x, mem, positions, mix_norm, w_in, mla_q_norm, mla_w_uq, mla_kv_norm, mla_w_ukv,
              hg_lower_bounds, hg_head_norm, gm_ln_g, gm_ln_b, gm_w_s, gm_b_s,
              w_branch_a, w_branch_b, w_branch_c, w_mix_out,
              xa_norm, mem_norm, xa_w_q, xa_w_kv, xa_w_o,
              ffn_norm, ffn_w_in, ffn_w_out, final_norm):
    inv_freq = ROPE_BASE ** (-jnp.arange(0, MLA_ROPE, 2, dtype=jnp.float32) / MLA_ROPE)
    ang = positions.astype(jnp.float32)[..., None] * inv_freq
    cos, sin = jnp.cos(ang), jnp.sin(ang)
    gamma = jax.nn.softmax(hg_lower_bounds.astype(jnp.float32), axis=0)
    cums = jnp.cumsum(gamma, axis=0)
    lower_bounds = cums - cums[0:1]

    for l in range(DEPTH):
        h = rms_norm(x, mix_norm[l])
        z = h @ w_in[l]
        c_q, c_kv, k_r, hq, hf, hi, hg, guv, gates = split_sizes(z, IN_SIZES)
        y_a = mla(c_q, c_kv, k_r, mla_q_norm[l], mla_w_uq[l], mla_kv_norm[l], mla_w_ukv[l], cos, sin) @ w_branch_a[l]
        y_b = hgrn2(hq, hf, hi, hg, lower_bounds[l], hg_head_norm[l]) @ w_branch_b[l]
        y_c = gmlp(guv, gm_ln_g[l], gm_ln_b[l], gm_w_s[l], gm_b_s[l]) @ w_branch_c[l]
        g_a, g_b, g_c = jnp.split(jax.nn.sigmoid(gates), 3, axis=-1)
        x = x + (g_a * y_a + g_b * y_b + g_c * y_c) @ w_mix_out[l]
        h = rms_norm(x, xa_norm[l])
        m = rms_norm(mem, mem_norm[l])
        x = x + cross_attn(h, m, xa_w_q[l], xa_w_kv[l], xa_w_o[l])
        h = rms_norm(x, ffn_norm[l])
        gt, up = jnp.split(h @ ffn_w_in[l], 2, axis=-1)
        x = x + (jax.nn.silu(gt) * up) @ ffn_w_out[l]
    return rms_norm(x, final_norm)
```

```python
import functools

import jax
import jax.numpy as jnp
from jax import lax
from jax.experimental import pallas as pl
from jax.experimental.pallas import tpu as pltpu

F32 = jnp.float32
BF16 = jnp.bfloat16

EPS = 1e-6
LANES = 128
VMEM_LIMIT = 56 * 1024 * 1024

MLA_HEADS = 8
MLA_Q_RANK = 384
MLA_KV_RANK = 256
MLA_NOPE = 64
MLA_ROPE = 32
MLA_V = 64
ROPE_BASE = 10000.0
HG_HEADS = 4
HG_KEY = 128
HG_CHUNK = 64
HG_SUB = 16
HG_W = HG_HEADS * HG_KEY
GM_GROUPS = 4
GM_CHUNK = 128
GM_W = 512
X_HEADS = 4
NEG_BIG = -1e30

SEG_MLA = MLA_Q_RANK + MLA_KV_RANK + 2 * LANES
SEG_HQIG = 3 * HG_W
SEG_HF = HG_W
SEG_GUV = 2 * GM_W
SEG_GATES = 3 * 1024


def _params(n_axes=1):
    return pltpu.CompilerParams(
        dimension_semantics=("arbitrary",) * n_axes, vmem_limit_bytes=VMEM_LIMIT)


def _rms(x, g):
    return x * lax.rsqrt(jnp.mean(x * x, axis=-1, keepdims=True) + EPS) * g


def _resident(shape):
    zeros = (0,) * len(shape)
    return pl.BlockSpec(shape, lambda *_: zeros, pipeline_mode=pl.Buffered(1))


def _rope_kernel(pos_ref, invf_ref, ct_ref, st_ref):
    ang = pos_ref[...] * invf_ref[...]
    lane = lax.broadcasted_iota(jnp.int32, ang.shape, 1)
    c, s = jnp.cos(ang), jnp.sin(ang)
    in_rope = (lane >= MLA_NOPE) & (lane < MLA_NOPE + MLA_ROPE)
    first = lane < MLA_NOPE + MLA_ROPE // 2
    ct_ref[...] = jnp.where(lane < MLA_NOPE, 1.0, jnp.where(in_rope, c, 0.0))
    st_ref[...] = jnp.where(in_rope, jnp.where(first, -s, s), 0.0)


def _rope_tables(pos_f, invf_lane, tm):
    t = pos_f.shape[0]
    return pl.pallas_call(
        _rope_kernel,
        grid=(t // tm,),
        in_specs=[pl.BlockSpec((tm, 1), lambda i: (i, 0)), _resident((1, LANES))],
        out_specs=[pl.BlockSpec((tm, LANES), lambda i: (i, 0))] * 2,
        out_shape=[jax.ShapeDtypeStruct((t, LANES), F32)] * 2,
        compiler_params=_params(),
        name="rope_tables",
    )(pos_f, invf_lane)


def _inproj_kernel(x_ref, g_ref, w_ref, mla_ref, hqig_ref, hf_ref, guv_ref, gates_ref, *, nchunk):
    h = _rms(x_ref[...], g_ref[...]).astype(BF16)
    start = 0
    for out in (mla_ref, hqig_ref, hf_ref, guv_ref, gates_ref):
        width = out.shape[1]
        for c0 in range(0, width, nchunk):
            c1 = min(c0 + nchunk, width)
            out[:, c0:c1] = jnp.dot(
                h, w_ref[:, start + c0:start + c1], preferred_element_type=F32).astype(out.dtype)
        start += width


def _inproj(x2, g, w_all, tm):
    t, d = x2.shape
    widths = (SEG_MLA, SEG_HQIG, SEG_HF, SEG_GUV, SEG_GATES)
    dtypes = (BF16, BF16, F32, BF16, BF16)
    return pl.pallas_call(
        functools.partial(_inproj_kernel, nchunk=512),
        grid=(t // tm,),
        in_specs=[pl.BlockSpec((tm, d), lambda i: (i, 0)), _resident((1, d)),
                  _resident(w_all.shape)],
        out_specs=[pl.BlockSpec((tm, w), lambda i: (i, 0)) for w in widths],
        out_shape=[jax.ShapeDtypeStruct((t, w), dt) for w, dt in zip(widths, dtypes)],
        compiler_params=_params(),
        name="in_proj",
    )(x2, g, w_all)


def _mla_prep_kernel(z_ref, ct_ref, st_ref, qg_ref, kvg_ref, wq_ref, wqr_ref, wk_ref, wv_ref,
                     q_ref, k_ref, v_ref, *, scale):
    ct, st = ct_ref[...], st_ref[...]
    cq = z_ref[:, :MLA_Q_RANK].astype(F32)
    qn = _rms(cq, qg_ref[...]).astype(BF16)
    q = jnp.dot(qn, wq_ref[...], preferred_element_type=F32)
    qr = jnp.dot(qn, wqr_ref[...], preferred_element_type=F32)
    ckv = z_ref[:, MLA_Q_RANK:MLA_Q_RANK + MLA_KV_RANK].astype(F32)
    kvn = _rms(ckv, kvg_ref[...]).astype(BF16)
    kn = jnp.dot(kvn, wk_ref[...], preferred_element_type=F32)
    v_ref[...] = jnp.dot(kvn, wv_ref[...], preferred_element_type=F32).astype(BF16)
    o = MLA_Q_RANK + MLA_KV_RANK
    krope = z_ref[:, o:o + LANES].astype(F32) * ct + z_ref[:, o + LANES:o + 2 * LANES].astype(F32) * st
    for h in range(MLA_HEADS):
        sl = slice(h * LANES, (h + 1) * LANES)
        q_ref[:, sl] = ((q[:, sl] * ct + qr[:, sl] * st) * scale).astype(BF16)
        k_ref[:, sl] = (kn[:, sl] + krope).astype(BF16)


def _mla_prep(z_mla, ct, st, qg, kvg, wq, wqr, wk, wv, tm):
    t = z_mla.shape[0]
    hw = MLA_HEADS * LANES
    row = lambda w: pl.BlockSpec((tm, w), lambda i: (i, 0))
    return pl.pallas_call(
        functools.partial(_mla_prep_kernel, scale=float((MLA_NOPE + MLA_ROPE) ** -0.5)),
        grid=(t // tm,),
        in_specs=[row(SEG_MLA), row(LANES), row(LANES), _resident(qg.shape), _resident(kvg.shape),
                  _resident(wq.shape), _resident(wqr.shape), _resident(wk.shape), _resident(wv.shape)],
        out_specs=[row(hw)] * 3,
        out_shape=[jax.ShapeDtypeStruct((t, hw), BF16)] * 3,
        compiler_params=_params(),
        name="mla_prep",
    )(z_mla, ct, st, qg, kvg, wq, wqr, wk, wv)


def _attn_kernel(q_ref, k_ref, v_ref, o_ref, *, tq):
    i = pl.program_id(2)
    q = q_ref[...]

    def scores(j):
        k = k_ref[pl.ds(pl.multiple_of(j * tq, tq), tq), :]
        return lax.dot_general(q, k, (((1,), (1,)), ((), ())), preferred_element_type=F32)

    def update(j, s, carry):
        m, l, acc = carry
        v = v_ref[pl.ds(pl.multiple_of(j * tq, tq), tq), :]
        m_new = jnp.maximum(m, jnp.max(s, axis=1, keepdims=True))
        alpha = jnp.exp(m - m_new)
        p = jnp.exp(s - m_new)
        l = alpha * l + jnp.sum(p, axis=1, keepdims=True)
        acc = alpha * acc + jnp.dot(p.astype(BF16), v, preferred_element_type=F32)
        return m_new, l, acc

    init = (jnp.full((tq, 1), NEG_BIG, F32), jnp.zeros((tq, 1), F32), jnp.zeros((tq, LANES), F32))
    carry = lax.fori_loop(0, i, lambda j, c: update(j, scores(j), c), init)
    row = lax.broadcasted_iota(jnp.int32, (tq, tq), 0)
    col = lax.broadcasted_iota(jnp.int32, (tq, tq), 1)
    s = jnp.where(col <= row, scores(i), NEG_BIG)
    _, l, acc = update(i, s, carry)
    o_ref[...] = (acc / l).astype(o_ref.dtype)


def _attention(q, k, v, batch, seq, tq):
    t, hw = q.shape
    nq = seq // tq
    qspec = pl.BlockSpec((tq, LANES), lambda b, h, i: (b * nq + i, h))
    kvspec = pl.BlockSpec((seq, LANES), lambda b, h, i: (b, h))
    return pl.pallas_call(
        functools.partial(_attn_kernel, tq=tq),
        grid=(batch, MLA_HEADS, nq),
        in_specs=[qspec, kvspec, kvspec],
        out_specs=qspec,
        out_shape=jax.ShapeDtypeStruct((t, hw), BF16),
        compiler_params=_params(3),
        name="mla_attention",
    )(q, k, v)


def _hgrn_kernel(lbp_ref, hn_ref, qig_ref, f_ref, o_ref, st_ref, b_scr, k_scr, v_scr, *,
                 layer, n_chunks):
    C, SUB, K = HG_CHUNK, HG_SUB, HG_KEY

    @pl.when(pl.program_id(1) == 0)
    def _():
        st_ref[...] = jnp.zeros_like(st_ref)

    lbp = lbp_ref[...]
    e = jnp.exp(lbp - jnp.max(lbp, axis=0, keepdims=True))
    gamma = e / jnp.sum(e, axis=0, keepdims=True)
    lb = jnp.zeros((1, HG_W), F32)
    for j in range(1, layer + 1):
        lb = lb + gamma[j:j + 1, :]
    log_lb, log_1mlb, one_m_lb = jnp.log(lb), jnp.log1p(-lb), 1.0 - lb
    hn = hn_ref[...]

    tri = (lax.broadcasted_iota(jnp.int32, (C, C), 1)
           <= lax.broadcasted_iota(jnp.int32, (C, C), 0)).astype(BF16)
    ones = jnp.ones((K, K), BF16)
    t_idx = lax.broadcasted_iota(jnp.int32, (SUB, K), 0)

    def chunk(c, carry):
        r0 = pl.multiple_of(c * C, C)
        rows = pl.ds(r0, C)
        fr = f_ref[rows, :]
        log_sig = jnp.minimum(fr, 0.0) - jnp.log1p(jnp.exp(-jnp.abs(fr)))
        y = log_1mlb + log_sig
        log_f = jnp.maximum(log_lb, y) + jnp.log1p(jnp.exp(-jnp.abs(log_lb - y)))
        kk = one_m_lb * jax.nn.sigmoid(-fr)
        p0 = log_f.astype(BF16)
        r1 = log_f - p0.astype(F32)
        p1 = r1.astype(BF16)
        p2 = (r1 - p1.astype(F32)).astype(BF16)
        b = (jnp.dot(tri, p0, preferred_element_type=F32)
             + jnp.dot(tri, p1, preferred_element_type=F32)
             + jnp.dot(tri, p2, preferred_element_type=F32))
        hq = qig_ref[rows, 0:HG_W].astype(F32)
        qq = hq * jax.nn.sigmoid(hq)
        vv = qig_ref[rows, HG_W:2 * HG_W]
        gg = qig_ref[rows, 2 * HG_W:3 * HG_W].astype(F32)
        b_scr[...] = b
        k_scr[...] = kk
        v_scr[...] = vv.astype(F32)

        for h in range(HG_HEADS):
            hs = slice(h * K, (h + 1) * K)
            bh, qh, kh, vh = b[:, hs], qq[:, hs], kk[:, hs], vv[:, hs]
            state = st_ref[h]
            o = lax.dot_general((qh * jnp.exp(bh)).astype(BF16), state.astype(BF16),
                                (((1,), (1,)), ((), ())), preferred_element_type=F32)
            blast = bh[C - 1:C, :]
            kdec = (kh * jnp.exp(blast - bh)).astype(BF16)
            st_ref[h] = state * jnp.exp(blast) + jnp.dot(
                vh.astype(F32).T.astype(BF16), kdec, preferred_element_type=F32)

            prods = []
            for i in range(C // SUB):
                r = i * SUB
                q_i, b_i = qh[r:r + SUB], bh[r:r + SUB]
                for s in range(SUB):
                    arg = jnp.where(t_idx >= s, b_i - b_scr[r + s:r + s + 1, hs], -jnp.inf)
                    prods.append((q_i * jnp.exp(arg) * k_scr[r + s:r + s + 1, hs]).astype(BF16))
            rsum = jnp.dot(jnp.concatenate(prods, axis=0), ones, preferred_element_type=F32)

            outs = []
            for i in range(C // SUB):
                r = i * SUB
                o_i = o[r:r + SUB]
                if i > 0:
                    bref = bh[r - 1:r, :]
                    qd = (qh[r:r + SUB] * jnp.exp(bh[r:r + SUB] - bref)).astype(BF16)
                    kd = (kh[0:r] * jnp.exp(bref - bh[0:r])).astype(BF16)
                    a = lax.dot_general(qd, kd, (((1,), (1,)), ((), ())), preferred_element_type=F32)
                    o_i = o_i + jnp.dot(a.astype(BF16), vh[0:r], preferred_element_type=F32)
                for s in range(SUB):
                    base = (i * SUB + s) * SUB
                    o_i = o_i + rsum[base:base + SUB] * v_scr[r + s:r + s + 1, hs]
                outs.append(o_i)
            o_full = jnp.concatenate(outs, axis=0)
            g_h = gg[:, hs]
            o_ref[rows, hs] = (_rms(o_full, hn) * (g_h * jax.nn.sigmoid(g_h))).astype(o_ref.dtype)
        return carry

    lax.fori_loop(0, n_chunks, chunk, 0)


def _hgrn(lbp, hn, z_hqig, z_hf, batch, seq, layer, tt):
    t = z_hf.shape[0]
    nt = seq // tt
    row = lambda w: pl.BlockSpec((tt, w), lambda b, i: (b * nt + i, 0))
    return pl.pallas_call(
        functools.partial(_hgrn_kernel, layer=layer, n_chunks=tt // HG_CHUNK),
        grid=(batch, nt),
        in_specs=[_resident(lbp.shape), _resident(hn.shape), row(SEG_HQIG), row(SEG_HF)],
        out_specs=row(HG_W),
        out_shape=jax.ShapeDtypeStruct((t, HG_W), BF16),
        scratch_shapes=[pltpu.VMEM((HG_HEADS, HG_KEY, HG_KEY), F32),
                        pltpu.VMEM((HG_CHUNK, HG_W), F32),
                        pltpu.VMEM((HG_CHUNK, HG_W), F32),
                        pltpu.VMEM((HG_CHUNK, HG_W), F32)],
        compiler_params=_params(2),
        name="hgrn2",
    )(lbp, hn, z_hqig, z_hf)


def _gmlp_kernel(z_ref, g_ref, b_ref, ws_ref, bias_ref, o_ref, *, n_chunks):
    z = z_ref[...].astype(F32)
    uv = 0.5 * z * (1.0 + lax.erf(z * (2.0 ** -0.5)))
    u, v = uv[:, :GM_W], uv[:, GM_W:]
    mu = jnp.mean(v, axis=-1, keepdims=True)
    var = jnp.mean(jnp.square(v - mu), axis=-1, keepdims=True)
    vn = ((v - mu) * lax.rsqrt(var + EPS) * g_ref[...] + b_ref[...]).astype(BF16)
    T = GM_CHUNK
    tri = lax.broadcasted_iota(jnp.int32, (T, T), 1) <= lax.broadcasted_iota(jnp.int32, (T, T), 0)
    for g in range(GM_GROUPS):
        gs = slice(g * LANES, (g + 1) * LANES)
        w = jnp.where(tri, ws_ref[g], 0.0).astype(BF16)
        bias = bias_ref[:, gs]
        for n in range(n_chunks):
            rs = slice(n * T, (n + 1) * T)
            mixed = jnp.dot(w, vn[rs, gs], preferred_element_type=F32) + bias
            o_ref[rs, gs] = (u[rs, gs] * mixed).astype(o_ref.dtype)


def _gmlp(z_guv, ln_g, ln_b, w_s, bias_full, tg):
    t = z_guv.shape[0]
    return pl.pallas_call(
        functools.partial(_gmlp_kernel, n_chunks=tg // GM_CHUNK),
        grid=(t // tg,),
        in_specs=[pl.BlockSpec((tg, SEG_GUV), lambda i: (i, 0)), _resident(ln_g.shape),
                  _resident(ln_b.shape), _resident(w_s.shape), _resident(bias_full.shape)],
        out_specs=pl.BlockSpec((tg, GM_W), lambda i: (i, 0)),
        out_shape=jax.ShapeDtypeStruct((t, GM_W), BF16),
        compiler_params=_params(),
        name="gmlp",
    )(z_guv, ln_g, ln_b, w_s, bias_full)


def _merge_kernel(x_ref, oa_ref, ob_ref, oc_ref, gates_ref, wa_ref, wb_ref, wc_ref, wm_ref, o_ref):
    d = x_ref.shape[1]
    m = None
    for idx, (br, w) in enumerate(((oa_ref, wa_ref), (ob_ref, wb_ref), (oc_ref, wc_ref))):
        y = jnp.dot(br[...], w[...], preferred_element_type=F32)
        term = jax.nn.sigmoid(gates_ref[:, idx * d:(idx + 1) * d].astype(F32)) * y
        m = term if m is None else m + term
    o_ref[...] = x_ref[...] + jnp.dot(m.astype(BF16), wm_ref[...], preferred_element_type=F32)


def _merge(x2, oa, ob, oc, gates, wa, wb, wc, wm, tm):
    t, d = x2.shape
    row = lambda w: pl.BlockSpec((tm, w), lambda i: (i, 0))
    return pl.pallas_call(
        _merge_kernel,
        grid=(t // tm,),
        in_specs=[row(d), row(oa.shape[1]), row(ob.shape[1]), row(oc.shape[1]), row(SEG_GATES),
                  _resident(wa.shape), _resident(wb.shape), _resident(wc.shape), _resident(wm.shape)],
        out_specs=row(d),
        out_shape=jax.ShapeDtypeStruct((t, d), F32),
        compiler_params=_params(),
        name="merge",
    )(x2, oa, ob, oc, gates, wa, wb, wc, wm)


def _memkv_kernel(m_ref, g_ref, w_ref, o_ref):
    h = _rms(m_ref[...], g_ref[...]).astype(BF16)
    o_ref[...] = jnp.dot(h, w_ref[...], preferred_element_type=F32).astype(o_ref.dtype)


def _memkv(mem2, g, w_kv):
    n, d = mem2.shape
    return pl.pallas_call(
        _memkv_kernel,
        grid=(1,),
        in_specs=[_resident(mem2.shape), _resident(g.shape), _resident(w_kv.shape)],
        out_specs=pl.BlockSpec((n, 2 * d), lambda i: (0, 0)),
        out_shape=jax.ShapeDtypeStruct((n, 2 * d), BF16),
        compiler_params=_params(),
        name="mem_kv",
    )(mem2, g, w_kv)


def _xattn_kernel(x_ref, g_ref, wq_ref, kv_ref, wo_ref, o_ref, *, scale):
    x = x_ref[...]
    d = x.shape[1]
    hd = d // X_HEADS
    h = _rms(x, g_ref[...]).astype(BF16)
    q = (jnp.dot(h, wq_ref[...], preferred_element_type=F32) * scale).astype(BF16)
    outs = []
    for hh in range(X_HEADS):
        k = kv_ref[:, hh * hd:(hh + 1) * hd]
        v = kv_ref[:, d + hh * hd:d + (hh + 1) * hd]
        s = lax.dot_general(q[:, hh * hd:(hh + 1) * hd], k, (((1,), (1,)), ((), ())),
                            preferred_element_type=F32)
        p = jnp.exp(s - jnp.max(s, axis=-1, keepdims=True))
        p = p / jnp.sum(p, axis=-1, keepdims=True)
        outs.append(jnp.dot(p.astype(BF16), v, preferred_element_type=F32).astype(BF16))
    o = jnp.concatenate(outs, axis=-1)
    o_ref[...] = x + jnp.dot(o, wo_ref[...], preferred_element_type=F32)


def _xattn(x2, g, wq, kv, wo, seq, n_mem, tm):
    t, d = x2.shape
    per_batch = seq // tm
    row = pl.BlockSpec((tm, d), lambda i: (i, 0))
    return pl.pallas_call(
        functools.partial(_xattn_kernel, scale=float((d // X_HEADS) ** -0.5)),
        grid=(t // tm,),
        in_specs=[row, _resident(g.shape), _resident(wq.shape),
                  pl.BlockSpec((n_mem, 2 * d), lambda i: (i // per_batch, 0)), _resident(wo.shape)],
        out_specs=row,
        out_shape=jax.ShapeDtypeStruct((t, d), F32),
        compiler_params=_params(),
        name="cross_attn",
    )(x2, g, wq, kv, wo)


def _ffn_kernel(x_ref, g_ref, wi_ref, wo_ref, fg_ref, o_ref, *, d_ff, n_split, final):
    x = x_ref[...]
    h = _rms(x, g_ref[...]).astype(BF16)
    acc = x
    cw = d_ff // n_split
    for c in range(n_split):
        gt = jnp.dot(h, wi_ref[:, c * cw:(c + 1) * cw], preferred_element_type=F32)
        up = jnp.dot(h, wi_ref[:, d_ff + c * cw:d_ff + (c + 1) * cw], preferred_element_type=F32)
        a = (gt * jax.nn.sigmoid(gt) * up).astype(BF16)
        acc = acc + jnp.dot(a, wo_ref[c * cw:(c + 1) * cw, :], preferred_element_type=F32)
    o_ref[...] = _rms(acc, fg_ref[...]) if final else acc


def _ffn(x2, g, w_in, w_out, final_g, final, tm):
    t, d = x2.shape
    d_ff = w_out.shape[0]
    row = pl.BlockSpec((tm, d), lambda i: (i, 0))
    return pl.pallas_call(
        functools.partial(_ffn_kernel, d_ff=d_ff, n_split=2, final=final),
        grid=(t // tm,),
        in_specs=[row, _resident(g.shape), _resident(w_in.shape), _resident(w_out.shape),
                  _resident(final_g.shape)],
        out_specs=row,
        out_shape=jax.ShapeDtypeStruct((t, d), F32),
        compiler_params=_params(),
        name="ffn",
    )(x2, g, w_in, w_out, final_g)


def _pad_heads(w, heads, width):
    r = w.shape[0]
    w = w.reshape(r, heads, width)
    return jnp.pad(w, ((0, 0), (0, 0), (0, LANES - width))).reshape(r, heads * LANES)


def _rope_slab(w_rope):
    half = MLA_ROPE // 2
    pad = ((0, 0), (0, 0), (MLA_NOPE, LANES - MLA_NOPE - MLA_ROPE))
    swapped = jnp.concatenate([w_rope[..., half:], w_rope[..., :half]], axis=-1)
    return jnp.pad(w_rope, pad), jnp.pad(swapped, pad)


def _layer_weights(l, w_in, mla_w_uq, mla_w_ukv, w_branch_a):
    w = w_in[l]
    d = w.shape[0]
    o = [0]
    for s in (MLA_Q_RANK, MLA_KV_RANK, MLA_ROPE, HG_W, HG_W, HG_W, HG_W, 2 * GM_W, 3 * d):
        o.append(o[-1] + s)
    c_q, c_kv, k_r, hq, hf, hi, hg, guv, gates = (w[:, o[i]:o[i + 1]] for i in range(9))
    kr, kr_rot = _rope_slab(k_r.reshape(d, 1, MLA_ROPE))
    w_all = jnp.concatenate(
        [c_q, c_kv, kr.reshape(d, LANES), kr_rot.reshape(d, LANES), hq, hi, hg, hf, guv, gates],
        axis=1).astype(BF16)

    uq = mla_w_uq[l].reshape(MLA_Q_RANK, MLA_HEADS, MLA_NOPE + MLA_ROPE)
    rope, rope_rot = _rope_slab(uq[..., MLA_NOPE:])
    wq = (jnp.pad(uq[..., :MLA_NOPE], ((0, 0), (0, 0), (0, LANES - MLA_NOPE))) + rope)
    hw = MLA_HEADS * LANES
    wq, wqr = wq.reshape(MLA_Q_RANK, hw).astype(BF16), rope_rot.reshape(MLA_Q_RANK, hw).astype(BF16)
    ukv = mla_w_ukv[l].reshape(MLA_KV_RANK, MLA_HEADS, MLA_NOPE + MLA_V)
    wk = _pad_heads(ukv[..., :MLA_NOPE].reshape(MLA_KV_RANK, -1), MLA_HEADS, MLA_NOPE).astype(BF16)
    wv = _pad_heads(ukv[..., MLA_NOPE:].reshape(MLA_KV_RANK, -1), MLA_HEADS, MLA_V).astype(BF16)
    wa = w_branch_a[l].reshape(MLA_HEADS, MLA_V, d)
    wa = jnp.pad(wa, ((0, 0), (0, LANES - MLA_V), (0, 0))).reshape(hw, d).astype(BF16)
    return w_all, wq, wqr, wk, wv, wa


def kernel(x, mem, positions, mix_norm, w_in, mla_q_norm, mla_w_uq, mla_kv_norm, mla_w_ukv,
           hg_lower_bounds, hg_head_norm, gm_ln_g, gm_ln_b, gm_w_s, gm_b_s,
           w_branch_a, w_branch_b, w_branch_c, w_mix_out,
           xa_norm, mem_norm, xa_w_q, xa_w_kv, xa_w_o,
           ffn_norm, ffn_w_in, ffn_w_out, final_norm):
    batch, seq, d = x.shape
    depth = w_in.shape[0]
    n_mem = mem.shape[1]
    t = batch * seq
    x2 = x.reshape(t, d)
    mem2 = mem.reshape(batch * n_mem, d)
    vec = lambda a: a.reshape(1, -1).astype(F32)

    tm = min(512, seq)
    inv_freq = ROPE_BASE ** (-jnp.arange(0, MLA_ROPE, 2, dtype=F32) / MLA_ROPE)
    invf_lane = jnp.zeros((LANES,), F32).at[MLA_NOPE:MLA_NOPE + MLA_ROPE].set(
        jnp.concatenate([inv_freq, inv_freq])).reshape(1, LANES)
    ct, st = _rope_tables(positions.astype(F32).reshape(t, 1), invf_lane, tm)

    for l in range(depth):
        w_all, wq, wqr, wk, wv, wa = _layer_weights(l, w_in, mla_w_uq, mla_w_ukv, w_branch_a)
        z_mla, z_hqig, z_hf, z_guv, z_gates = _inproj(x2, vec(mix_norm[l]), w_all, min(256, seq))
        q, k, v = _mla_prep(z_mla, ct, st, vec(mla_q_norm[l]), vec(mla_kv_norm[l]), wq, wqr, wk, wv, tm)
        oa = _attention(q, k, v, batch, seq, tm)
        ob = _hgrn(hg_lower_bounds.astype(F32), vec(hg_head_norm[l]), z_hqig, z_hf, batch, seq, l, tm)
        bias_full = jnp.repeat(gm_b_s[l].T, GM_W // GM_GROUPS, axis=1)
        oc = _gmlp(z_guv, vec(gm_ln_g[l]), vec(gm_ln_b[l]), gm_w_s[l], bias_full, tm)
        x2 = _merge(x2, oa, ob, oc, z_gates, wa, w_branch_b[l].astype(BF16),
                    w_branch_c[l].astype(BF16), w_mix_out[l].astype(BF16), tm)
        kv = _memkv(mem2, vec(mem_norm[l]), xa_w_kv[l].astype(BF16))
        x2 = _xattn(x2, vec(xa_norm[l]), xa_w_q[l].astype(BF16), kv, xa_w_o[l].astype(BF16),
                    seq, n_mem, tm)
        x2 = _ffn(x2, vec(ffn_norm[l]), ffn_w_in[l].astype(BF16), ffn_w_out[l].astype(BF16),
                  vec(final_norm), l == depth - 1, tm)
    return x2.reshape(batch, seq, d)
```

```python
import functools
import math

import jax
import jax.numpy as jnp
from jax import lax
from jax.experimental import pallas as pl
from jax.experimental.pallas import tpu as pltpu

F32 = jnp.float32
BF16 = jnp.bfloat16

EPS = 1e-6
LANES = 128
VMEM_LIMIT = 56 * 1024 * 1024

MLA_HEADS = 8
MLA_Q_RANK = 384
MLA_KV_RANK = 256
MLA_NOPE = 64
MLA_ROPE = 32
MLA_V = 64
ROPE_BASE = 10000.0
HG_HEADS = 4
HG_KEY = 128
HG_CHUNK = 64
HG_SUB = 16
HG_W = HG_HEADS * HG_KEY
GM_GROUPS = 4
GM_CHUNK = 128
GM_W = 512
X_HEADS = 4
NEG_BIG = -1e30

SEG_MLA = MLA_Q_RANK + MLA_KV_RANK + 2 * LANES
SEG_HQIG = 3 * HG_W
SEG_HF = HG_W
SEG_GUV = 2 * GM_W
SEG_GATES = 3 * 1024


def _params(n_axes=1):
    return pltpu.CompilerParams(
        dimension_semantics=("arbitrary",) * n_axes, vmem_limit_bytes=VMEM_LIMIT)


def _rms(x, g):
    return x * lax.rsqrt(jnp.mean(x * x, axis=-1, keepdims=True) + EPS) * g


def _resident(shape):
    zeros = (0,) * len(shape)
    return pl.BlockSpec(shape, lambda *_: zeros, pipeline_mode=pl.Buffered(1))


def _rope_kernel(pos_ref, invf_ref, ct_ref, st_ref):
    ang = pos_ref[...] * invf_ref[...]
    lane = lax.broadcasted_iota(jnp.int32, ang.shape, 1)
    c, s = jnp.cos(ang), jnp.sin(ang)
    in_rope = (lane >= MLA_NOPE) & (lane < MLA_NOPE + MLA_ROPE)
    first = lane < MLA_NOPE + MLA_ROPE // 2
    ct_ref[...] = jnp.where(lane < MLA_NOPE, 1.0, jnp.where(in_rope, c, 0.0))
    st_ref[...] = jnp.where(in_rope, jnp.where(first, -s, s), 0.0)


def _rope_tables(pos_f, invf_lane, tm):
    t = pos_f.shape[0]
    return pl.pallas_call(
        _rope_kernel,
        grid=(t // tm,),
        in_specs=[pl.BlockSpec((tm, 1), lambda i: (i, 0)), _resident((1, LANES))],
        out_specs=[pl.BlockSpec((tm, LANES), lambda i: (i, 0))] * 2,
        out_shape=[jax.ShapeDtypeStruct((t, LANES), F32)] * 2,
        compiler_params=_params(),
        name="rope_tables",
    )(pos_f, invf_lane)


def _inproj_kernel(x_ref, g_ref, w_ref, mla_ref, hqig_ref, hf_ref, guv_ref, gates_ref, *, nchunk):
    h = _rms(x_ref[...], g_ref[...]).astype(BF16)
    start = 0
    for out in (mla_ref, hqig_ref, hf_ref, guv_ref, gates_ref):
        width = out.shape[1]
        for c0 in range(0, width, nchunk):
            c1 = min(c0 + nchunk, width)
            out[:, c0:c1] = jnp.dot(
                h, w_ref[:, start + c0:start + c1], preferred_element_type=F32).astype(out.dtype)
        start += width


def _inproj(x2, g, w_all, tm):
    t, d = x2.shape
    widths = (SEG_MLA, SEG_HQIG, SEG_HF, SEG_GUV, SEG_GATES)
    dtypes = (BF16, BF16, F32, BF16, BF16)
    return pl.pallas_call(
        functools.partial(_inproj_kernel, nchunk=512),
        grid=(t // tm,),
        in_specs=[pl.BlockSpec((tm, d), lambda i: (i, 0)), _resident((1, d)),
                  _resident(w_all.shape)],
        out_specs=[pl.BlockSpec((tm, w), lambda i: (i, 0)) for w in widths],
        out_shape=[jax.ShapeDtypeStruct((t, w), dt) for w, dt in zip(widths, dtypes)],
        compiler_params=_params(),
        name="in_proj",
    )(x2, g, w_all)


def _mla_prep_kernel(z_ref, ct_ref, st_ref, qg_ref, kvg_ref, wq_ref, wqr_ref, wk_ref, wv_ref,
                     q_ref, k_ref, v_ref, *, scale):
    ct, st = ct_ref[...], st_ref[...]
    cq = z_ref[:, :MLA_Q_RANK].astype(F32)
    qn = _rms(cq, qg_ref[...]).astype(BF16)
    q = jnp.dot(qn, wq_ref[...], preferred_element_type=F32)
    qr = jnp.dot(qn, wqr_ref[...], preferred_element_type=F32)
    ckv = z_ref[:, MLA_Q_RANK:MLA_Q_RANK + MLA_KV_RANK].astype(F32)
    kvn = _rms(ckv, kvg_ref[...]).astype(BF16)
    kn = jnp.dot(kvn, wk_ref[...], preferred_element_type=F32)
    v = jnp.dot(kvn, wv_ref[...], preferred_element_type=F32)
    lane = lax.broadcasted_iota(jnp.int32, v.shape, 1)
    v_ref[...] = jnp.where(lane % LANES == MLA_V, 1.0, v).astype(BF16)
    o = MLA_Q_RANK + MLA_KV_RANK
    krope = z_ref[:, o:o + LANES].astype(F32) * ct + z_ref[:, o + LANES:o + 2 * LANES].astype(F32) * st
    for h in range(MLA_HEADS):
        sl = slice(h * LANES, (h + 1) * LANES)
        q_ref[:, sl] = ((q[:, sl] * ct + qr[:, sl] * st) * scale).astype(BF16)
        k_ref[:, sl] = (kn[:, sl] + krope).astype(BF16)


def _mla_prep(z_mla, ct, st, qg, kvg, wq, wqr, wk, wv, tm):
    t = z_mla.shape[0]
    hw = MLA_HEADS * LANES
    row = lambda w: pl.BlockSpec((tm, w), lambda i: (i, 0))
    return pl.pallas_call(
        functools.partial(_mla_prep_kernel, scale=float((MLA_NOPE + MLA_ROPE) ** -0.5 * math.log2(math.e))),
        grid=(t // tm,),
        in_specs=[row(SEG_MLA), row(LANES), row(LANES), _resident(qg.shape), _resident(kvg.shape),
                  _resident(wq.shape), _resident(wqr.shape), _resident(wk.shape), _resident(wv.shape)],
        out_specs=[row(hw)] * 3,
        out_shape=[jax.ShapeDtypeStruct((t, hw), BF16)] * 3,
        compiler_params=_params(),
        name="mla_prep",
    )(z_mla, ct, st, qg, kvg, wq, wqr, wk, wv)


def _attn_kernel(q_ref, k_ref, v_ref, o_ref, s0, s1, p0, p1, al0, al1, m_scr, acc_scr, *, tq):
    tk = tq // 2
    i = pl.program_id(2)
    q = q_ref[...]
    diff = (lax.broadcasted_iota(jnp.int32, (tq, tk), 1)
            - lax.broadcasted_iota(jnp.int32, (tq, tk), 0))

    def kv_rows(j):
        return pl.ds(pl.multiple_of(j * tk, tk), tk)

    def scores(j, dst):
        dst[...] = lax.dot_general(q, k_ref[kv_rows(j), :], (((1,), (1,)), ((), ())),
                                   preferred_element_type=F32)

    def accumulate(j, p_ref, al_ref):
        acc_scr[...] = al_ref[...] * acc_scr[...] + jnp.dot(
            p_ref[...], v_ref[kv_rows(j), :], preferred_element_type=F32)

    def softmax(s_ref, p_ref, al_ref, diag_offset=None):
        s = s_ref[...]
        if diag_offset is not None:
            s = jnp.where(diff <= diag_offset, s, NEG_BIG)
        m = m_scr[...]
        m_new = jnp.maximum(m, jnp.max(s, axis=1, keepdims=True))
        p_ref[...] = jnp.concatenate(
            [jnp.exp2(s[:, c:c + LANES] - m_new) for c in range(0, tk, LANES)], axis=1).astype(BF16)
        al_ref[...] = jnp.exp2(m - m_new)
        m_scr[...] = m_new

    m_scr[...] = jnp.full_like(m_scr, NEG_BIG)
    acc_scr[...] = jnp.zeros_like(acc_scr)
    p1[...] = jnp.zeros_like(p1)
    al1[...] = jnp.ones_like(al1)
    scores(0, s0)

    def pair(p, carry):
        a = 2 * p
        scores(a + 1, s1)
        accumulate(jnp.maximum(a - 1, 0), p1, al1)
        softmax(s0, p0, al0)
        scores(a + 2, s0)
        accumulate(a, p0, al0)
        softmax(s1, p1, al1)
        return carry

    lax.fori_loop(0, i, pair, 0)
    a = 2 * i
    scores(a + 1, s1)
    accumulate(jnp.maximum(a - 1, 0), p1, al1)
    softmax(s0, p0, al0, 0)
    accumulate(a, p0, al0)
    softmax(s1, p1, al1, -tk)
    accumulate(a + 1, p1, al1)
    acc = acc_scr[...]
    o_ref[...] = (acc / acc[:, MLA_V:MLA_V + 1]).astype(o_ref.dtype)


def _attention(q, k, v, batch, seq, tq):
    t, hw = q.shape
    nq = seq // tq
    tk = tq // 2
    qspec = pl.BlockSpec((tq, LANES), lambda b, h, i: (b * nq + i, h))
    kvspec = pl.BlockSpec((seq, LANES), lambda b, h, i: (b, h))
    return pl.pallas_call(
        functools.partial(_attn_kernel, tq=tq),
        grid=(batch, MLA_HEADS, nq),
        in_specs=[qspec, kvspec, kvspec],
        out_specs=qspec,
        out_shape=jax.ShapeDtypeStruct((t, hw), BF16),
        scratch_shapes=[pltpu.VMEM((tq, tk), F32), pltpu.VMEM((tq, tk), F32),
                        pltpu.VMEM((tq, tk), BF16), pltpu.VMEM((tq, tk), BF16),
                        pltpu.VMEM((tq, LANES), F32), pltpu.VMEM((tq, LANES), F32),
                        pltpu.VMEM((tq, LANES), F32), pltpu.VMEM((tq, LANES), F32)],
        compiler_params=_params(3),
        name="mla_attention",
    )(q, k, v)


def _hgrn_kernel(lbp_ref, hn_ref, qig_ref, f_ref, o_ref, st_ref, b_scr, k_scr, v_scr, *,
                 layer, n_chunks):
    C, SUB, K = HG_CHUNK, HG_SUB, HG_KEY

    @pl.when(pl.program_id(1) == 0)
    def _():
        st_ref[...] = jnp.zeros_like(st_ref)

    lbp = lbp_ref[...]
    e = jnp.exp(lbp - jnp.max(lbp, axis=0, keepdims=True))
    gamma = e / jnp.sum(e, axis=0, keepdims=True)
    lb = jnp.zeros((1, HG_W), F32)
    for j in range(1, layer + 1):
        lb = lb + gamma[j:j + 1, :]
    log_lb, log_1mlb, one_m_lb = jnp.log(lb), jnp.log1p(-lb), 1.0 - lb
    hn = hn_ref[...]

    tri = (lax.broadcasted_iota(jnp.int32, (C, C), 1)
           <= lax.broadcasted_iota(jnp.int32, (C, C), 0)).astype(BF16)
    ones = jnp.ones((K, K), BF16)
    t_idx = lax.broadcasted_iota(jnp.int32, (SUB, K), 0)

    def chunk(c, carry):
        r0 = pl.multiple_of(c * C, C)
        rows = pl.ds(r0, C)
        fr = f_ref[rows, :]
        log_sig = jnp.minimum(fr, 0.0) - jnp.log1p(jnp.exp(-jnp.abs(fr)))
        y = log_1mlb + log_sig
        log_f = jnp.maximum(log_lb, y) + jnp.log1p(jnp.exp(-jnp.abs(log_lb - y)))
        kk = one_m_lb * jax.nn.sigmoid(-fr)
        p0 = log_f.astype(BF16)
        r1 = log_f - p0.astype(F32)
        p1 = r1.astype(BF16)
        p2 = (r1 - p1.astype(F32)).astype(BF16)
        b = (jnp.dot(tri, p0, preferred_element_type=F32)
             + jnp.dot(tri, p1, preferred_element_type=F32)
             + jnp.dot(tri, p2, preferred_element_type=F32))
        hq = qig_ref[rows, 0:HG_W].astype(F32)
        qq = hq * jax.nn.sigmoid(hq)
        vv = qig_ref[rows, HG_W:2 * HG_W]
        gg = qig_ref[rows, 2 * HG_W:3 * HG_W].astype(F32)
        b_scr[...] = b
        k_scr[...] = kk
        v_scr[...] = vv.astype(F32)

        for h in range(HG_HEADS):
            hs = slice(h * K, (h + 1) * K)
            bh, qh, kh, vh = b[:, hs], qq[:, hs], kk[:, hs], vv[:, hs]
            state = st_ref[h]
            o = lax.dot_general((qh * jnp.exp(bh)).astype(BF16), state.astype(BF16),
                                (((1,), (1,)), ((), ())), preferred_element_type=F32)
            blast = bh[C - 1:C, :]
            kdec = (kh * jnp.exp(blast - bh)).astype(BF16)
            st_ref[h] = state * jnp.exp(blast) + jnp.dot(
                vh.astype(F32).T.astype(BF16), kdec, preferred_element_type=F32)

            prods = []
            for i in range(C // SUB):
                r = i * SUB
                q_i, b_i = qh[r:r + SUB], bh[r:r + SUB]
                for s in range(SUB):
                    arg = jnp.where(t_idx >= s, b_i - b_scr[r + s:r + s + 1, hs], -jnp.inf)
                    prods.append((q_i * jnp.exp(arg) * k_scr[r + s:r + s + 1, hs]).astype(BF16))
            rsum = jnp.dot(jnp.concatenate(prods, axis=0), ones, preferred_element_type=F32)

            outs = []
            for i in range(C // SUB):
                r = i * SUB
                o_i = o[r:r + SUB]
                if i > 0:
                    bref = bh[r - 1:r, :]
                    qd = (qh[r:r + SUB] * jnp.exp(bh[r:r + SUB] - bref)).astype(BF16)
                    kd = (kh[0:r] * jnp.exp(bref - bh[0:r])).astype(BF16)
                    a = lax.dot_general(qd, kd, (((1,), (1,)), ((), ())), preferred_element_type=F32)
                    o_i = o_i + jnp.dot(a.astype(BF16), vh[0:r], preferred_element_type=F32)
                for s in range(SUB):
                    base = (i * SUB + s) * SUB
                    o_i = o_i + rsum[base:base + SUB] * v_scr[r + s:r + s + 1, hs]
                outs.append(o_i)
            o_full = jnp.concatenate(outs, axis=0)
            g_h = gg[:, hs]
            o_ref[rows, hs] = (_rms(o_full, hn) * (g_h * jax.nn.sigmoid(g_h))).astype(o_ref.dtype)
        return carry

    lax.fori_loop(0, n_chunks, chunk, 0)


def _hgrn(lbp, hn, z_hqig, z_hf, batch, seq, layer, tt):
    t = z_hf.shape[0]
    nt = seq // tt
    row = lambda w: pl.BlockSpec((tt, w), lambda b, i: (b * nt + i, 0))
    return pl.pallas_call(
        functools.partial(_hgrn_kernel, layer=layer, n_chunks=tt // HG_CHUNK),
        grid=(batch, nt),
        in_specs=[_resident(lbp.shape), _resident(hn.shape), row(SEG_HQIG), row(SEG_HF)],
        out_specs=row(HG_W),
        out_shape=jax.ShapeDtypeStruct((t, HG_W), BF16),
        scratch_shapes=[pltpu.VMEM((HG_HEADS, HG_KEY, HG_KEY), F32),
                        pltpu.VMEM((HG_CHUNK, HG_W), F32),
                        pltpu.VMEM((HG_CHUNK, HG_W), F32),
                        pltpu.VMEM((HG_CHUNK, HG_W), F32)],
        compiler_params=_params(2),
        name="hgrn2",
    )(lbp, hn, z_hqig, z_hf)


def _gmlp_kernel(z_ref, g_ref, b_ref, ws_ref, bias_ref, o_ref, *, n_chunks):
    z = z_ref[...].astype(F32)
    uv = 0.5 * z * (1.0 + lax.erf(z * (2.0 ** -0.5)))
    u, v = uv[:, :GM_W], uv[:, GM_W:]
    mu = jnp.mean(v, axis=-1, keepdims=True)
    var = jnp.mean(jnp.square(v - mu), axis=-1, keepdims=True)
    vn = ((v - mu) * lax.rsqrt(var + EPS) * g_ref[...] + b_ref[...]).astype(BF16)
    T = GM_CHUNK
    tri = lax.broadcasted_iota(jnp.int32, (T, T), 1) <= lax.broadcasted_iota(jnp.int32, (T, T), 0)
    for g in range(GM_GROUPS):
        gs = slice(g * LANES, (g + 1) * LANES)
        w = jnp.where(tri, ws_ref[g], 0.0).astype(BF16)
        bias = bias_ref[:, gs]
        for n in range(n_chunks):
            rs = slice(n * T, (n + 1) * T)
            mixed = jnp.dot(w, vn[rs, gs], preferred_element_type=F32) + bias
            o_ref[rs, gs] = (u[rs, gs] * mixed).astype(o_ref.dtype)


def _gmlp(z_guv, ln_g, ln_b, w_s, bias_full, tg):
    t = z_guv.shape[0]
    return pl.pallas_call(
        functools.partial(_gmlp_kernel, n_chunks=tg // GM_CHUNK),
        grid=(t // tg,),
        in_specs=[pl.BlockSpec((tg, SEG_GUV), lambda i: (i, 0)), _resident(ln_g.shape),
                  _resident(ln_b.shape), _resident(w_s.shape), _resident(bias_full.shape)],
        out_specs=pl.BlockSpec((tg, GM_W), lambda i: (i, 0)),
        out_shape=jax.ShapeDtypeStruct((t, GM_W), BF16),
        compiler_params=_params(),
        name="gmlp",
    )(z_guv, ln_g, ln_b, w_s, bias_full)


def _merge_kernel(x_ref, oa_ref, ob_ref, oc_ref, gates_ref, wa_ref, wb_ref, wc_ref, wm_ref, o_ref):
    d = x_ref.shape[1]
    m = None
    for idx, (br, w) in enumerate(((oa_ref, wa_ref), (ob_ref, wb_ref), (oc_ref, wc_ref))):
        y = jnp.dot(br[...], w[...], preferred_element_type=F32)
        term = jax.nn.sigmoid(gates_ref[:, idx * d:(idx + 1) * d].astype(F32)) * y
        m = term if m is None else m + term
    o_ref[...] = x_ref[...] + jnp.dot(m.astype(BF16), wm_ref[...], preferred_element_type=F32)


def _merge(x2, oa, ob, oc, gates, wa, wb, wc, wm, tm):
    t, d = x2.shape
    row = lambda w: pl.BlockSpec((tm, w), lambda i: (i, 0))
    return pl.pallas_call(
        _merge_kernel,
        grid=(t // tm,),
        in_specs=[row(d), row(oa.shape[1]), row(ob.shape[1]), row(oc.shape[1]), row(SEG_GATES),
                  _resident(wa.shape), _resident(wb.shape), _resident(wc.shape), _resident(wm.shape)],
        out_specs=row(d),
        out_shape=jax.ShapeDtypeStruct((t, d), F32),
        compiler_params=_params(),
        name="merge",
    )(x2, oa, ob, oc, gates, wa, wb, wc, wm)


def _memkv_kernel(m_ref, g_ref, w_ref, o_ref):
    h = _rms(m_ref[...], g_ref[...]).astype(BF16)
    o_ref[...] = jnp.dot(h, w_ref[...], preferred_element_type=F32).astype(o_ref.dtype)


def _memkv(mem2, g, w_kv):
    n, d = mem2.shape
    return pl.pallas_call(
        _memkv_kernel,
        grid=(1,),
        in_specs=[_resident(mem2.shape), _resident(g.shape), _resident(w_kv.shape)],
        out_specs=pl.BlockSpec((n, 2 * d), lambda i: (0, 0)),
        out_shape=jax.ShapeDtypeStruct((n, 2 * d), BF16),
        compiler_params=_params(),
        name="mem_kv",
    )(mem2, g, w_kv)


def _xattn_kernel(x_ref, g_ref, wq_ref, kv_ref, wo_ref, o_ref, *, scale):
    x = x_ref[...]
    d = x.shape[1]
    hd = d // X_HEADS
    h = _rms(x, g_ref[...]).astype(BF16)
    q = (jnp.dot(h, wq_ref[...], preferred_element_type=F32) * scale).astype(BF16)
    outs = []
    for hh in range(X_HEADS):
        k = kv_ref[:, hh * hd:(hh + 1) * hd]
        v = kv_ref[:, d + hh * hd:d + (hh + 1) * hd]
        s = lax.dot_general(q[:, hh * hd:(hh + 1) * hd], k, (((1,), (1,)), ((), ())),
                            preferred_element_type=F32)
        p = jnp.exp(s - jnp.max(s, axis=-1, keepdims=True))
        p = p / jnp.sum(p, axis=-1, keepdims=True)
        outs.append(jnp.dot(p.astype(BF16), v, preferred_element_type=F32).astype(BF16))
    o = jnp.concatenate(outs, axis=-1)
    o_ref[...] = x + jnp.dot(o, wo_ref[...], preferred_element_type=F32)


def _xattn(x2, g, wq, kv, wo, seq, n_mem, tm):
    t, d = x2.shape
    per_batch = seq // tm
    row = pl.BlockSpec((tm, d), lambda i: (i, 0))
    return pl.pallas_call(
        functools.partial(_xattn_kernel, scale=float((d // X_HEADS) ** -0.5)),
        grid=(t // tm,),
        in_specs=[row, _resident(g.shape), _resident(wq.shape),
                  pl.BlockSpec((n_mem, 2 * d), lambda i: (i // per_batch, 0)), _resident(wo.shape)],
        out_specs=row,
        out_shape=jax.ShapeDtypeStruct((t, d), F32),
        compiler_params=_params(),
        name="cross_attn",
    )(x2, g, wq, kv, wo)


def _ffn_kernel(x_ref, g_ref, wi_ref, wo_ref, fg_ref, o_ref, *, d_ff, n_split, final):
    x = x_ref[...]
    h = _rms(x, g_ref[...]).astype(BF16)
    acc = x
    cw = d_ff // n_split
    for c in range(n_split):
        gt = jnp.dot(h, wi_ref[:, c * cw:(c + 1) * cw], preferred_element_type=F32)
        up = jnp.dot(h, wi_ref[:, d_ff + c * cw:d_ff + (c + 1) * cw], preferred_element_type=F32)
        a = (gt * jax.nn.sigmoid(gt) * up).astype(BF16)
        acc = acc + jnp.dot(a, wo_ref[c * cw:(c + 1) * cw, :], preferred_element_type=F32)
    o_ref[...] = _rms(acc, fg_ref[...]) if final else acc


def _ffn(x2, g, w_in, w_out, final_g, final, tm):
    t, d = x2.shape
    d_ff = w_out.shape[0]
    row = pl.BlockSpec((tm, d), lambda i: (i, 0))
    return pl.pallas_call(
        functools.partial(_ffn_kernel, d_ff=d_ff, n_split=2, final=final),
        grid=(t // tm,),
        in_specs=[row, _resident(g.shape), _resident(w_in.shape), _resident(w_out.shape),
                  _resident(final_g.shape)],
        out_specs=row,
        out_shape=jax.ShapeDtypeStruct((t, d), F32),
        compiler_params=_params(),
        name="ffn",
    )(x2, g, w_in, w_out, final_g)


def _pad_heads(w, heads, width):
    r = w.shape[0]
    w = w.reshape(r, heads, width)
    return jnp.pad(w, ((0, 0), (0, 0), (0, LANES - width))).reshape(r, heads * LANES)


def _rope_slab(w_rope):
    half = MLA_ROPE // 2
    pad = ((0, 0), (0, 0), (MLA_NOPE, LANES - MLA_NOPE - MLA_ROPE))
    swapped = jnp.concatenate([w_rope[..., half:], w_rope[..., :half]], axis=-1)
    return jnp.pad(w_rope, pad), jnp.pad(swapped, pad)


def _layer_weights(l, w_in, mla_w_uq, mla_w_ukv, w_branch_a):
    w = w_in[l]
    d = w.shape[0]
    o = [0]
    for s in (MLA_Q_RANK, MLA_KV_RANK, MLA_ROPE, HG_W, HG_W, HG_W, HG_W, 2 * GM_W, 3 * d):
        o.append(o[-1] + s)
    c_q, c_kv, k_r, hq, hf, hi, hg, guv, gates = (w[:, o[i]:o[i + 1]] for i in range(9))
    kr, kr_rot = _rope_slab(k_r.reshape(d, 1, MLA_ROPE))
    w_all = jnp.concatenate(
        [c_q, c_kv, kr.reshape(d, LANES), kr_rot.reshape(d, LANES), hq, hi, hg, hf, guv, gates],
        axis=1).astype(BF16)

    uq = mla_w_uq[l].reshape(MLA_Q_RANK, MLA_HEADS, MLA_NOPE + MLA_ROPE)
    rope, rope_rot = _rope_slab(uq[..., MLA_NOPE:])
    wq = (jnp.pad(uq[..., :MLA_NOPE], ((0, 0), (0, 0), (0, LANES - MLA_NOPE))) + rope)
    hw = MLA_HEADS * LANES
    wq, wqr = wq.reshape(MLA_Q_RANK, hw).astype(BF16), rope_rot.reshape(MLA_Q_RANK, hw).astype(BF16)
    ukv = mla_w_ukv[l].reshape(MLA_KV_RANK, MLA_HEADS, MLA_NOPE + MLA_V)
    wk = _pad_heads(ukv[..., :MLA_NOPE].reshape(MLA_KV_RANK, -1), MLA_HEADS, MLA_NOPE).astype(BF16)
    wv = _pad_heads(ukv[..., MLA_NOPE:].reshape(MLA_KV_RANK, -1), MLA_HEADS, MLA_V).astype(BF16)
    wa = w_branch_a[l].reshape(MLA_HEADS, MLA_V, d)
    wa = jnp.pad(wa, ((0, 0), (0, LANES - MLA_V), (0, 0))).reshape(hw, d).astype(BF16)
    return w_all, wq, wqr, wk, wv, wa


def kernel(x, mem, positions, mix_norm, w_in, mla_q_norm, mla_w_uq, mla_kv_norm, mla_w_ukv,
           hg_lower_bounds, hg_head_norm, gm_ln_g, gm_ln_b, gm_w_s, gm_b_s,
           w_branch_a, w_branch_b, w_branch_c, w_mix_out,
           xa_norm, mem_norm, xa_w_q, xa_w_kv, xa_w_o,
           ffn_norm, ffn_w_in, ffn_w_out, final_norm):
    batch, seq, d = x.shape
    depth = w_in.shape[0]
    n_mem = mem.shape[1]
    t = batch * seq
    x2 = x.reshape(t, d)
    mem2 = mem.reshape(batch * n_mem, d)
    vec = lambda a: a.reshape(1, -1).astype(F32)

    tm = min(512, seq)
    inv_freq = ROPE_BASE ** (-jnp.arange(0, MLA_ROPE, 2, dtype=F32) / MLA_ROPE)
    invf_lane = jnp.zeros((LANES,), F32).at[MLA_NOPE:MLA_NOPE + MLA_ROPE].set(
        jnp.concatenate([inv_freq, inv_freq])).reshape(1, LANES)
    ct, st = _rope_tables(positions.astype(F32).reshape(t, 1), invf_lane, tm)

    for l in range(depth):
        w_all, wq, wqr, wk, wv, wa = _layer_weights(l, w_in, mla_w_uq, mla_w_ukv, w_branch_a)
        z_mla, z_hqig, z_hf, z_guv, z_gates = _inproj(x2, vec(mix_norm[l]), w_all, min(256, seq))
        q, k, v = _mla_prep(z_mla, ct, st, vec(mla_q_norm[l]), vec(mla_kv_norm[l]), wq, wqr, wk, wv, tm)
        oa = _attention(q, k, v, batch, seq, min(1024, seq))
        ob = _hgrn(hg_lower_bounds.astype(F32), vec(hg_head_norm[l]), z_hqig, z_hf, batch, seq, l, tm)
        bias_full = jnp.repeat(gm_b_s[l].T, GM_W // GM_GROUPS, axis=1)
        oc = _gmlp(z_guv, vec(gm_ln_g[l]), vec(gm_ln_b[l]), gm_w_s[l], bias_full, tm)
        x2 = _merge(x2, oa, ob, oc, z_gates, wa, w_branch_b[l].astype(BF16),
                    w_branch_c[l].astype(BF16), w_mix_out[l].astype(BF16), tm)
        kv = _memkv(mem2, vec(mem_norm[l]), xa_w_kv[l].astype(BF16))
        x2 = _xattn(x2, vec(xa_norm[l]), xa_w_q[l].astype(BF16), kv, xa_w_o[l].astype(BF16),
                    seq, n_mem, tm)
        x2 = _ffn(x2, vec(ffn_norm[l]), ffn_w_in[l].astype(BF16), ffn_w_out[l].astype(BF16),
                  vec(final_norm), l == depth - 1, tm)
    return x2.reshape(batch, seq, d)
```

```python
import functools
import math

import jax
import jax.numpy as jnp
from jax import lax
from jax.experimental import pallas as pl
from jax.experimental.pallas import tpu as pltpu

F32 = jnp.float32
BF16 = jnp.bfloat16

EPS = 1e-6
LANES = 128
SUBLANES = 8
LOG2E = math.log2(math.e)
VMEM_LIMIT = 56 * 1024 * 1024

MLA_HEADS = 8
MLA_Q_RANK = 384
MLA_KV_RANK = 256
MLA_NOPE = 64
MLA_ROPE = 32
MLA_V = 64
ROPE_BASE = 10000.0
HG_HEADS = 4
HG_KEY = 128
HG_CHUNK = 64
HG_SUB = 16
HG_W = HG_HEADS * HG_KEY
GM_GROUPS = 4
GM_CHUNK = 128
GM_W = 512
X_HEADS = 4
NEG_BIG = -1e30

SEG_MLA = MLA_Q_RANK + MLA_KV_RANK + 2 * LANES
SEG_HQIG = 3 * HG_W
SEG_HF = HG_W
SEG_GUV = 2 * GM_W
SEG_GATES = 3 * 1024


def _params(n_axes=1):
    return pltpu.CompilerParams(
        dimension_semantics=("arbitrary",) * n_axes, vmem_limit_bytes=VMEM_LIMIT)


def _rms(x, g):
    return x * lax.rsqrt(jnp.mean(x * x, axis=-1, keepdims=True) + EPS) * g


def _resident(shape):
    zeros = (0,) * len(shape)
    return pl.BlockSpec(shape, lambda *_: zeros, pipeline_mode=pl.Buffered(1))


def _rope_kernel(pos_ref, invf_ref, ct_ref, st_ref):
    ang = pos_ref[...] * invf_ref[...]
    lane = lax.broadcasted_iota(jnp.int32, ang.shape, 1)
    c, s = jnp.cos(ang), jnp.sin(ang)
    in_rope = (lane >= MLA_NOPE) & (lane < MLA_NOPE + MLA_ROPE)
    first = lane < MLA_NOPE + MLA_ROPE // 2
    ct_ref[...] = jnp.where(lane < MLA_NOPE, 1.0, jnp.where(in_rope, c, 0.0))
    st_ref[...] = jnp.where(in_rope, jnp.where(first, -s, s), 0.0)


def _rope_tables(pos_f, invf_lane, tm):
    t = pos_f.shape[0]
    return pl.pallas_call(
        _rope_kernel,
        grid=(t // tm,),
        in_specs=[pl.BlockSpec((tm, 1), lambda i: (i, 0)), _resident((1, LANES))],
        out_specs=[pl.BlockSpec((tm, LANES), lambda i: (i, 0))] * 2,
        out_shape=[jax.ShapeDtypeStruct((t, LANES), F32)] * 2,
        compiler_params=_params(),
        name="rope_tables",
    )(pos_f, invf_lane)


def _inproj_kernel(x_ref, g_ref, w_ref, mla_ref, hqig_ref, hf_ref, guv_ref, gates_ref, *, nchunk):
    h = _rms(x_ref[...], g_ref[...]).astype(BF16)
    start = 0
    for out in (mla_ref, hqig_ref, hf_ref, guv_ref, gates_ref):
        width = out.shape[1]
        for c0 in range(0, width, nchunk):
            c1 = min(c0 + nchunk, width)
            out[:, c0:c1] = jnp.dot(
                h, w_ref[:, start + c0:start + c1], preferred_element_type=F32).astype(out.dtype)
        start += width


def _inproj(x2, g, w_all, tm):
    t, d = x2.shape
    widths = (SEG_MLA, SEG_HQIG, SEG_HF, SEG_GUV, SEG_GATES)
    dtypes = (BF16, BF16, F32, BF16, BF16)
    return pl.pallas_call(
        functools.partial(_inproj_kernel, nchunk=512),
        grid=(t // tm,),
        in_specs=[pl.BlockSpec((tm, d), lambda i: (i, 0)), _resident((1, d)),
                  _resident(w_all.shape)],
        out_specs=[pl.BlockSpec((tm, w), lambda i: (i, 0)) for w in widths],
        out_shape=[jax.ShapeDtypeStruct((t, w), dt) for w, dt in zip(widths, dtypes)],
        compiler_params=_params(),
        name="in_proj",
    )(x2, g, w_all)


def _mla_prep_kernel(z_ref, ct_ref, st_ref, qg_ref, kvg_ref, wq_ref, wqr_ref, wk_ref, wv_ref,
                     q_ref, k_ref, v_ref, *, scale):
    ct, st = ct_ref[...], st_ref[...]
    cq = z_ref[:, :MLA_Q_RANK].astype(F32)
    qn = _rms(cq, qg_ref[...]).astype(BF16)
    q = jnp.dot(qn, wq_ref[...], preferred_element_type=F32)
    qr = jnp.dot(qn, wqr_ref[...], preferred_element_type=F32)
    ckv = z_ref[:, MLA_Q_RANK:MLA_Q_RANK + MLA_KV_RANK].astype(F32)
    kvn = _rms(ckv, kvg_ref[...]).astype(BF16)
    kn = jnp.dot(kvn, wk_ref[...], preferred_element_type=F32)
    v = jnp.dot(kvn, wv_ref[...], preferred_element_type=F32)
    lane = lax.broadcasted_iota(jnp.int32, v.shape, 1)
    v_ref[...] = jnp.where(lane % LANES == MLA_V, 1.0, v).astype(BF16)
    o = MLA_Q_RANK + MLA_KV_RANK
    krope = z_ref[:, o:o + LANES].astype(F32) * ct + z_ref[:, o + LANES:o + 2 * LANES].astype(F32) * st
    for h in range(MLA_HEADS):
        sl = slice(h * LANES, (h + 1) * LANES)
        q_ref[:, sl] = ((q[:, sl] * ct + qr[:, sl] * st) * scale).astype(BF16)
        k_ref[:, sl] = (kn[:, sl] + krope).astype(BF16)


def _mla_prep(z_mla, ct, st, qg, kvg, wq, wqr, wk, wv, tm):
    t = z_mla.shape[0]
    hw = MLA_HEADS * LANES
    row = lambda w: pl.BlockSpec((tm, w), lambda i: (i, 0))
    return pl.pallas_call(
        functools.partial(_mla_prep_kernel, scale=float((MLA_NOPE + MLA_ROPE) ** -0.5 * math.log2(math.e))),
        grid=(t // tm,),
        in_specs=[row(SEG_MLA), row(LANES), row(LANES), _resident(qg.shape), _resident(kvg.shape),
                  _resident(wq.shape), _resident(wqr.shape), _resident(wk.shape), _resident(wv.shape)],
        out_specs=[row(hw)] * 3,
        out_shape=[jax.ShapeDtypeStruct((t, hw), BF16)] * 3,
        compiler_params=_params(),
        name="mla_prep",
    )(z_mla, ct, st, qg, kvg, wq, wqr, wk, wv)


def _attn_kernel(q_ref, k_ref, v_ref, o_ref, s0, s1, p0, p1, al0, al1, m_scr, acc_scr, *, tq):
    tk = tq // 2
    i = pl.program_id(2)
    every, lo, hi = slice(0, tq), slice(0, tk), slice(tk, tq)
    causal = (lax.broadcasted_iota(jnp.int32, (tk, tk), 1)
              <= lax.broadcasted_iota(jnp.int32, (tk, tk), 0))

    def kv_rows(j):
        return pl.ds(pl.multiple_of(j * tk, tk), tk)

    def scores(j, dst, rows=every):
        dst[rows, :] = lax.dot_general(q_ref[rows, :], k_ref[kv_rows(j), :],
                                       (((1,), (1,)), ((), ())), preferred_element_type=F32)

    def accumulate(j, p_ref, al_ref, rows=every):
        acc_scr[rows, :] = al_ref[rows, :] * acc_scr[rows, :] + jnp.dot(
            p_ref[rows, :], v_ref[kv_rows(j), :], preferred_element_type=F32)

    def softmax(s_ref, p_ref, al_ref, rows=every, masked=False):
        s = s_ref[rows, :]
        if masked:
            s = jnp.where(causal, s, NEG_BIG)
        m = m_scr[rows, :]
        m_new = jnp.maximum(m, jnp.max(s, axis=1, keepdims=True))
        p_ref[rows, :] = jnp.concatenate(
            [jnp.exp2(s[:, c:c + LANES] - m_new) for c in range(0, tk, LANES)], axis=1).astype(BF16)
        al_ref[rows, :] = jnp.exp2(m - m_new)
        m_scr[rows, :] = m_new

    m_scr[...] = jnp.full_like(m_scr, NEG_BIG)
    acc_scr[...] = jnp.zeros_like(acc_scr)
    p1[...] = jnp.zeros_like(p1)
    al1[...] = jnp.ones_like(al1)
    scores(0, s0)

    def pair(p):
        a = 2 * p
        scores(a + 1, s1)
        accumulate(jnp.maximum(a - 1, 0), p1, al1)
        softmax(s0, p0, al0)
        scores(a + 2, s0)
        accumulate(a, p0, al0)
        softmax(s1, p1, al1)

    def two_pairs(t, carry):
        pair(2 * t)
        pair(2 * t + 1)
        return carry

    lax.fori_loop(0, i // 2, two_pairs, 0)

    @pl.when(i % 2 == 1)
    def _():
        pair(i - 1)

    a = 2 * i
    scores(a + 1, s1, hi)
    accumulate(jnp.maximum(a - 1, 0), p1, al1)
    softmax(s0, p0, al0, lo, masked=True)
    softmax(s0, p0, al0, hi)
    accumulate(a, p0, al0)
    softmax(s1, p1, al1, hi, masked=True)
    accumulate(a + 1, p1, al1, hi)
    acc = acc_scr[...]
    o_ref[...] = (acc / acc[:, MLA_V:MLA_V + 1]).astype(o_ref.dtype)


def _attention(q, k, v, batch, seq, tq):
    t, hw = q.shape
    nq = seq // tq
    tk = tq // 2
    qspec = pl.BlockSpec((tq, LANES), lambda b, h, i: (b * nq + i, h))
    kvspec = pl.BlockSpec((seq, LANES), lambda b, h, i: (b, h))
    return pl.pallas_call(
        functools.partial(_attn_kernel, tq=tq),
        grid=(batch, MLA_HEADS, nq),
        in_specs=[qspec, kvspec, kvspec],
        out_specs=qspec,
        out_shape=jax.ShapeDtypeStruct((t, hw), BF16),
        scratch_shapes=[pltpu.VMEM((tq, tk), F32), pltpu.VMEM((tq, tk), F32),
                        pltpu.VMEM((tq, tk), BF16), pltpu.VMEM((tq, tk), BF16),
                        pltpu.VMEM((tq, LANES), F32), pltpu.VMEM((tq, LANES), F32),
                        pltpu.VMEM((tq, LANES), F32), pltpu.VMEM((tq, LANES), F32)],
        compiler_params=_params(3),
        name="mla_attention",
    )(q, k, v)


def _hgrn_kernel(lbp_ref, hn_ref, qig_ref, f_ref, o_ref, st_ref, b_scr, k_scr, q_scr, v_scr, o_scr,
                 prod_scr, rsum_scr, *, layer, n_chunks):
    C, SUB, K, HALF = HG_CHUNK, HG_SUB, HG_KEY, SUBLANES

    @pl.when(pl.program_id(1) == 0)
    def _():
        st_ref[...] = jnp.zeros_like(st_ref)

    fr = f_ref[...]
    e = jnp.exp(-jnp.abs(fr))
    inv = 1.0 / (1.0 + e)
    log_sig = jnp.minimum(fr, 0.0) + jnp.log(inv)
    sig_neg = jnp.where(fr >= 0.0, e * inv, inv)
    if layer == 0:
        log_f, kk = log_sig, sig_neg
    else:
        lbp = lbp_ref[...]
        w = jnp.exp(lbp - jnp.max(lbp, axis=0, keepdims=True))
        gamma = w / jnp.sum(w, axis=0, keepdims=True)
        lb = gamma[1:2, :]
        for j in range(2, layer + 1):
            lb = lb + gamma[j:j + 1, :]
        log_lb = jnp.log(lb)
        y = jnp.log1p(-lb) + log_sig
        log_f = jnp.maximum(log_lb, y) + jnp.log1p(jnp.exp(-jnp.abs(log_lb - y)))
        kk = (1.0 - lb) * sig_neg
    hq = qig_ref[:, 0:HG_W].astype(F32)
    qq = hq * jax.nn.sigmoid(hq)
    vv = qig_ref[:, HG_W:2 * HG_W].astype(F32)
    log_f = log_f * LOG2E
    p0 = log_f.astype(BF16)
    r1 = log_f - p0.astype(F32)
    p1 = r1.astype(BF16)
    p2 = (r1 - p1.astype(F32)).astype(BF16)
    tri = (lax.broadcasted_iota(jnp.int32, (C, C), 1)
           <= lax.broadcasted_iota(jnp.int32, (C, C), 0)).astype(BF16)
    for c in range(n_chunks):
        rs = slice(c * C, (c + 1) * C)
        b_scr[c] = (jnp.dot(tri, p0[rs], preferred_element_type=F32)
                    + jnp.dot(tri, p1[rs], preferred_element_type=F32)
                    + jnp.dot(tri, p2[rs], preferred_element_type=F32))
        k_scr[c], q_scr[c], v_scr[c] = kk[rs], qq[rs], vv[rs]

    ones = jnp.ones((K, K), BF16)
    t_idx = lax.broadcasted_iota(jnp.int32, (HALF, K), 0)
    keep = [t_idx >= s for s in range(HALF)]

    def chunk(c, carry):
        heads = [slice(h * K, (h + 1) * K) for h in range(HG_HEADS)]
        blocks = range(0, C, SUB)
        bs = [b_scr[c, :, hs] for hs in heads]
        qs = [q_scr[c, :, hs] for hs in heads]
        ks = [k_scr[c, :, hs] for hs in heads]
        vs = [qig_ref[pl.ds(pl.multiple_of(c * C, C), C), HG_W + h * K:HG_W + (h + 1) * K]
              for h in range(HG_HEADS)]

        o_state = []
        for h, hs in enumerate(heads):
            state = st_ref[h]
            o_state.append(lax.dot_general(
                (qs[h] * jnp.exp2(bs[h])).astype(BF16), state.astype(BF16),
                (((1,), (1,)), ((), ())), preferred_element_type=F32))
            blast = bs[h][C - 1:C, :]
            kdec = (ks[h] * jnp.exp2(blast - bs[h])).astype(BF16)
            st_ref[h] = state * jnp.exp2(blast) + jnp.dot(
                v_scr[c, :, hs].T.astype(BF16), kdec, preferred_element_type=F32)

        a_off = {}
        for h in range(HG_HEADS):
            for r in blocks[1:]:
                bref = bs[h][r - 1:r, :]
                qd = (qs[h][r:r + SUB] * jnp.exp2(bs[h][r:r + SUB] - bref)).astype(BF16)
                kd = (ks[h][0:r] * jnp.exp2(bref - bs[h][0:r])).astype(BF16)
                a_off[h, r] = lax.dot_general(qd, kd, (((1,), (1,)), ((), ())),
                                              preferred_element_type=F32)

        owner = []
        for h, hs in enumerate(heads):
            for r in blocks:
                prods = []
                for s in range(SUB):
                    b_row, k_row = b_scr[c, r + s:r + s + 1, hs], k_scr[c, r + s:r + s + 1, hs]
                    for half in range(s // HALF, SUB // HALF):
                        t0 = r + half * HALF
                        arg = bs[h][t0:t0 + HALF] - b_row
                        if half == s // HALF:
                            arg = jnp.where(keep[s % HALF], arg, -jnp.inf)
                        prods.append(qs[h][t0:t0 + HALF] * jnp.exp2(arg) * k_row)
                        owner.append((h, t0, r + s))
                n0 = (len(owner) - len(prods)) * HALF
                prod_scr[n0:n0 + len(prods) * HALF, :] = jnp.concatenate(prods, axis=0).astype(BF16)
        rsum_scr[...] = jnp.dot(prod_scr[...], ones, preferred_element_type=F32)

        o_half = {}
        for h in range(HG_HEADS):
            for r in blocks:
                o_i = o_state[h][r:r + SUB]
                if r > 0:
                    o_i = o_i + jnp.dot(a_off[h, r].astype(BF16), vs[h][0:r],
                                        preferred_element_type=F32)
                for half in range(SUB // HALF):
                    o_half[h, r + half * HALF] = o_i[half * HALF:(half + 1) * HALF]
        for n, (h, t0, s_row) in enumerate(owner):
            o_half[h, t0] = o_half[h, t0] + (rsum_scr[n * HALF:(n + 1) * HALF, :]
                                             * v_scr[c, s_row:s_row + 1, heads[h]])
        for h, hs in enumerate(heads):
            o_scr[c, :, hs] = jnp.concatenate([o_half[h, t0] for t0 in range(0, C, HALF)], axis=0)
        return carry

    lax.fori_loop(0, n_chunks, chunk, 0)

    hn = hn_ref[...]
    for c in range(n_chunks):
        rs = slice(c * C, (c + 1) * C)
        for h in range(HG_HEADS):
            hs = slice(h * K, (h + 1) * K)
            g_h = qig_ref[rs, 2 * HG_W + h * K:2 * HG_W + (h + 1) * K].astype(F32)
            o_ref[rs, hs] = (_rms(o_scr[c, :, hs], hn)
                             * (g_h * jax.nn.sigmoid(g_h))).astype(o_ref.dtype)


def _hgrn(lbp, hn, z_hqig, z_hf, batch, seq, layer, tt):
    t = z_hf.shape[0]
    nt = seq // tt
    row = lambda w: pl.BlockSpec((tt, w), lambda b, i: (b * nt + i, 0))
    tile = pltpu.VMEM((tt // HG_CHUNK, HG_CHUNK, HG_W), F32)
    halves = HG_SUB // SUBLANES
    pair_rows = HG_HEADS * (HG_CHUNK // HG_SUB) * (halves * (halves + 1) // 2 * SUBLANES) * SUBLANES
    return pl.pallas_call(
        functools.partial(_hgrn_kernel, layer=layer, n_chunks=tt // HG_CHUNK),
        grid=(batch, nt),
        in_specs=[_resident(lbp.shape), _resident(hn.shape), row(SEG_HQIG), row(SEG_HF)],
        out_specs=row(HG_W),
        out_shape=jax.ShapeDtypeStruct((t, HG_W), BF16),
        scratch_shapes=[pltpu.VMEM((HG_HEADS, HG_KEY, HG_KEY), F32), tile, tile, tile, tile, tile,
                        pltpu.VMEM((pair_rows, HG_KEY), BF16), pltpu.VMEM((pair_rows, HG_KEY), F32)],
        compiler_params=_params(2),
        name="hgrn2",
    )(lbp, hn, z_hqig, z_hf)


def _gmlp_kernel(z_ref, g_ref, b_ref, ws_ref, bias_ref, o_ref, *, n_chunks):
    z = z_ref[...].astype(F32)
    uv = 0.5 * z * (1.0 + lax.erf(z * (2.0 ** -0.5)))
    u, v = uv[:, :GM_W], uv[:, GM_W:]
    mu = jnp.mean(v, axis=-1, keepdims=True)
    var = jnp.mean(jnp.square(v - mu), axis=-1, keepdims=True)
    vn = ((v - mu) * lax.rsqrt(var + EPS) * g_ref[...] + b_ref[...]).astype(BF16)
    T = GM_CHUNK
    tri = lax.broadcasted_iota(jnp.int32, (T, T), 1) <= lax.broadcasted_iota(jnp.int32, (T, T), 0)
    for g in range(GM_GROUPS):
        gs = slice(g * LANES, (g + 1) * LANES)
        w = jnp.where(tri, ws_ref[g], 0.0).astype(BF16)
        bias = bias_ref[:, gs]
        for n in range(n_chunks):
            rs = slice(n * T, (n + 1) * T)
            mixed = jnp.dot(w, vn[rs, gs], preferred_element_type=F32) + bias
            o_ref[rs, gs] = (u[rs, gs] * mixed).astype(o_ref.dtype)


def _gmlp(z_guv, ln_g, ln_b, w_s, bias_full, tg):
    t = z_guv.shape[0]
    return pl.pallas_call(
        functools.partial(_gmlp_kernel, n_chunks=tg // GM_CHUNK),
        grid=(t // tg,),
        in_specs=[pl.BlockSpec((tg, SEG_GUV), lambda i: (i, 0)), _resident(ln_g.shape),
                  _resident(ln_b.shape), _resident(w_s.shape), _resident(bias_full.shape)],
        out_specs=pl.BlockSpec((tg, GM_W), lambda i: (i, 0)),
        out_shape=jax.ShapeDtypeStruct((t, GM_W), BF16),
        compiler_params=_params(),
        name="gmlp",
    )(z_guv, ln_g, ln_b, w_s, bias_full)


def _merge_kernel(x_ref, oa_ref, ob_ref, oc_ref, gates_ref, wa_ref, wb_ref, wc_ref, wm_ref, o_ref):
    d = x_ref.shape[1]
    m = None
    for idx, (br, w) in enumerate(((oa_ref, wa_ref), (ob_ref, wb_ref), (oc_ref, wc_ref))):
        y = jnp.dot(br[...], w[...], preferred_element_type=F32)
        term = jax.nn.sigmoid(gates_ref[:, idx * d:(idx + 1) * d].astype(F32)) * y
        m = term if m is None else m + term
    o_ref[...] = x_ref[...] + jnp.dot(m.astype(BF16), wm_ref[...], preferred_element_type=F32)


def _merge(x2, oa, ob, oc, gates, wa, wb, wc, wm, tm):
    t, d = x2.shape
    row = lambda w: pl.BlockSpec((tm, w), lambda i: (i, 0))
    return pl.pallas_call(
        _merge_kernel,
        grid=(t // tm,),
        in_specs=[row(d), row(oa.shape[1]), row(ob.shape[1]), row(oc.shape[1]), row(SEG_GATES),
                  _resident(wa.shape), _resident(wb.shape), _resident(wc.shape), _resident(wm.shape)],
        out_specs=row(d),
        out_shape=jax.ShapeDtypeStruct((t, d), F32),
        compiler_params=_params(),
        name="merge",
    )(x2, oa, ob, oc, gates, wa, wb, wc, wm)


def _memkv_kernel(m_ref, g_ref, w_ref, o_ref):
    h = _rms(m_ref[...], g_ref[...]).astype(BF16)
    o_ref[...] = jnp.dot(h, w_ref[...], preferred_element_type=F32).astype(o_ref.dtype)


def _memkv(mem2, g, w_kv):
    n, d = mem2.shape
    return pl.pallas_call(
        _memkv_kernel,
        grid=(1,),
        in_specs=[_resident(mem2.shape), _resident(g.shape), _resident(w_kv.shape)],
        out_specs=pl.BlockSpec((n, 2 * d), lambda i: (0, 0)),
        out_shape=jax.ShapeDtypeStruct((n, 2 * d), BF16),
        compiler_params=_params(),
        name="mem_kv",
    )(mem2, g, w_kv)


def _xattn_kernel(x_ref, g_ref, wq_ref, kv_ref, wo_ref, o_ref, *, scale):
    x = x_ref[...]
    d = x.shape[1]
    hd = d // X_HEADS
    h = _rms(x, g_ref[...]).astype(BF16)
    q = (jnp.dot(h, wq_ref[...], preferred_element_type=F32) * scale).astype(BF16)
    outs = []
    for hh in range(X_HEADS):
        k = kv_ref[:, hh * hd:(hh + 1) * hd]
        v = kv_ref[:, d + hh * hd:d + (hh + 1) * hd]
        s = lax.dot_general(q[:, hh * hd:(hh + 1) * hd], k, (((1,), (1,)), ((), ())),
                            preferred_element_type=F32)
        p = jnp.exp(s - jnp.max(s, axis=-1, keepdims=True))
        p = p / jnp.sum(p, axis=-1, keepdims=True)
        outs.append(jnp.dot(p.astype(BF16), v, preferred_element_type=F32).astype(BF16))
    o = jnp.concatenate(outs, axis=-1)
    o_ref[...] = x + jnp.dot(o, wo_ref[...], preferred_element_type=F32)


def _xattn(x2, g, wq, kv, wo, seq, n_mem, tm):
    t, d = x2.shape
    per_batch = seq // tm
    row = pl.BlockSpec((tm, d), lambda i: (i, 0))
    return pl.pallas_call(
        functools.partial(_xattn_kernel, scale=float((d // X_HEADS) ** -0.5)),
        grid=(t // tm,),
        in_specs=[row, _resident(g.shape), _resident(wq.shape),
                  pl.BlockSpec((n_mem, 2 * d), lambda i: (i // per_batch, 0)), _resident(wo.shape)],
        out_specs=row,
        out_shape=jax.ShapeDtypeStruct((t, d), F32),
        compiler_params=_params(),
        name="cross_attn",
    )(x2, g, wq, kv, wo)


def _ffn_kernel(x_ref, g_ref, wi_ref, wo_ref, fg_ref, o_ref, *, d_ff, n_split, final):
    x = x_ref[...]
    h = _rms(x, g_ref[...]).astype(BF16)
    acc = x
    cw = d_ff // n_split
    for c in range(n_split):
        gt = jnp.dot(h, wi_ref[:, c * cw:(c + 1) * cw], preferred_element_type=F32)
        up = jnp.dot(h, wi_ref[:, d_ff + c * cw:d_ff + (c + 1) * cw], preferred_element_type=F32)
        a = (gt * jax.nn.sigmoid(gt) * up).astype(BF16)
        acc = acc + jnp.dot(a, wo_ref[c * cw:(c + 1) * cw, :], preferred_element_type=F32)
    o_ref[...] = _rms(acc, fg_ref[...]) if final else acc


def _ffn(x2, g, w_in, w_out, final_g, final, tm):
    t, d = x2.shape
    d_ff = w_out.shape[0]
    row = pl.BlockSpec((tm, d), lambda i: (i, 0))
    return pl.pallas_call(
        functools.partial(_ffn_kernel, d_ff=d_ff, n_split=2, final=final),
        grid=(t // tm,),
        in_specs=[row, _resident(g.shape), _resident(w_in.shape), _resident(w_out.shape),
                  _resident(final_g.shape)],
        out_specs=row,
        out_shape=jax.ShapeDtypeStruct((t, d), F32),
        compiler_params=_params(),
        name="ffn",
    )(x2, g, w_in, w_out, final_g)


def _pad_heads(w, heads, width):
    r = w.shape[0]
    w = w.reshape(r, heads, width)
    return jnp.pad(w, ((0, 0), (0, 0), (0, LANES - width))).reshape(r, heads * LANES)


def _rope_slab(w_rope):
    half = MLA_ROPE // 2
    pad = ((0, 0), (0, 0), (MLA_NOPE, LANES - MLA_NOPE - MLA_ROPE))
    swapped = jnp.concatenate([w_rope[..., half:], w_rope[..., :half]], axis=-1)
    return jnp.pad(w_rope, pad), jnp.pad(swapped, pad)


def _layer_weights(l, w_in, mla_w_uq, mla_w_ukv, w_branch_a):
    w = w_in[l]
    d = w.shape[0]
    o = [0]
    for s in (MLA_Q_RANK, MLA_KV_RANK, MLA_ROPE, HG_W, HG_W, HG_W, HG_W, 2 * GM_W, 3 * d):
        o.append(o[-1] + s)
    c_q, c_kv, k_r, hq, hf, hi, hg, guv, gates = (w[:, o[i]:o[i + 1]] for i in range(9))
    kr, kr_rot = _rope_slab(k_r.reshape(d, 1, MLA_ROPE))
    w_all = jnp.concatenate(
        [c_q, c_kv, kr.reshape(d, LANES), kr_rot.reshape(d, LANES), hq, hi, hg, hf, guv, gates],
        axis=1).astype(BF16)

    uq = mla_w_uq[l].reshape(MLA_Q_RANK, MLA_HEADS, MLA_NOPE + MLA_ROPE)
    rope, rope_rot = _rope_slab(uq[..., MLA_NOPE:])
    wq = (jnp.pad(uq[..., :MLA_NOPE], ((0, 0), (0, 0), (0, LANES - MLA_NOPE))) + rope)
    hw = MLA_HEADS * LANES
    wq, wqr = wq.reshape(MLA_Q_RANK, hw).astype(BF16), rope_rot.reshape(MLA_Q_RANK, hw).astype(BF16)
    ukv = mla_w_ukv[l].reshape(MLA_KV_RANK, MLA_HEADS, MLA_NOPE + MLA_V)
    wk = _pad_heads(ukv[..., :MLA_NOPE].reshape(MLA_KV_RANK, -1), MLA_HEADS, MLA_NOPE).astype(BF16)
    wv = _pad_heads(ukv[..., MLA_NOPE:].reshape(MLA_KV_RANK, -1), MLA_HEADS, MLA_V).astype(BF16)
    wa = w_branch_a[l].reshape(MLA_HEADS, MLA_V, d)
    wa = jnp.pad(wa, ((0, 0), (0, LANES - MLA_V), (0, 0))).reshape(hw, d).astype(BF16)
    return w_all, wq, wqr, wk, wv, wa


def kernel(x, mem, positions, mix_norm, w_in, mla_q_norm, mla_w_uq, mla_kv_norm, mla_w_ukv,
           hg_lower_bounds, hg_head_norm, gm_ln_g, gm_ln_b, gm_w_s, gm_b_s,
           w_branch_a, w_branch_b, w_branch_c, w_mix_out,
           xa_norm, mem_norm, xa_w_q, xa_w_kv, xa_w_o,
           ffn_norm, ffn_w_in, ffn_w_out, final_norm):
    batch, seq, d = x.shape
    depth = w_in.shape[0]
    n_mem = mem.shape[1]
    t = batch * seq
    x2 = x.reshape(t, d)
    mem2 = mem.reshape(batch * n_mem, d)
    vec = lambda a: a.reshape(1, -1).astype(F32)

    tm = min(512, seq)
    inv_freq = ROPE_BASE ** (-jnp.arange(0, MLA_ROPE, 2, dtype=F32) / MLA_ROPE)
    invf_lane = jnp.zeros((LANES,), F32).at[MLA_NOPE:MLA_NOPE + MLA_ROPE].set(
        jnp.concatenate([inv_freq, inv_freq])).reshape(1, LANES)
    ct, st = _rope_tables(positions.astype(F32).reshape(t, 1), invf_lane, tm)

    for l in range(depth):
        w_all, wq, wqr, wk, wv, wa = _layer_weights(l, w_in, mla_w_uq, mla_w_ukv, w_branch_a)
        z_mla, z_hqig, z_hf, z_guv, z_gates = _inproj(x2, vec(mix_norm[l]), w_all, min(256, seq))
        q, k, v = _mla_prep(z_mla, ct, st, vec(mla_q_norm[l]), vec(mla_kv_norm[l]), wq, wqr, wk, wv, tm)
        oa = _attention(q, k, v, batch, seq, tm)
        ob = _hgrn(hg_lower_bounds.astype(F32), vec(hg_head_norm[l]), z_hqig, z_hf, batch, seq, l, tm)
        bias_full = jnp.repeat(gm_b_s[l].T, GM_W // GM_GROUPS, axis=1)
        oc = _gmlp(z_guv, vec(gm_ln_g[l]), vec(gm_ln_b[l]), gm_w_s[l], bias_full, tm)
        x2 = _merge(x2, oa, ob, oc, z_gates, wa, w_branch_b[l].astype(BF16),
                    w_branch_c[l].astype(BF16), w_mix_out[l].astype(BF16), tm)
        kv = _memkv(mem2, vec(mem_norm[l]), xa_w_kv[l].astype(BF16))
        x2 = _xattn(x2, vec(xa_norm[l]), xa_w_q[l].astype(BF16), kv, xa_w_o[l].astype(BF16),
                    seq, n_mem, tm)
        x2 = _ffn(x2, vec(ffn_norm[l]), ffn_w_in[l].astype(BF16), ffn_w_out[l].astype(BF16),
                  vec(final_norm), l == depth - 1, tm)
    return x2.reshape(batch, seq, d)
```

```python
import functools
import math

import jax
import jax.numpy as jnp
from jax import lax
from jax.experimental import pallas as pl
from jax.experimental.pallas import tpu as pltpu

F32 = jnp.float32
BF16 = jnp.bfloat16

EPS = 1e-6
LANES = 128
SUBLANES = 8
MXU_TILE = 256
LOG2E = math.log2(math.e)
VMEM_LIMIT = 56 * 1024 * 1024

MLA_HEADS = 8
MLA_Q_RANK = 384
MLA_KV_RANK = 256
MLA_NOPE = 64
MLA_ROPE = 32
MLA_V = 64
ROPE_BASE = 10000.0
HG_HEADS = 4
HG_KEY = 128
HG_CHUNK = 64
HG_SUB = 16
HG_W = HG_HEADS * HG_KEY
GM_GROUPS = 4
GM_CHUNK = 128
GM_W = 512
X_HEADS = 4
NEG_BIG = -1e30
ATTN_HEADS_PER_STEP = 2
ATTN_PAIRS_PER_TRIP = 2

SEG_MLA = MLA_Q_RANK + MLA_KV_RANK + 2 * LANES
SEG_HQIG = 3 * HG_W
SEG_HF = HG_W
SEG_GUV = 2 * GM_W
SEG_GATES = 3 * 1024


def _params(n_axes=1):
    return pltpu.CompilerParams(
        dimension_semantics=("arbitrary",) * n_axes, vmem_limit_bytes=VMEM_LIMIT)


def _rms(x, g):
    return x * lax.rsqrt(jnp.mean(x * x, axis=-1, keepdims=True) + EPS) * g


def _resident(shape):
    zeros = (0,) * len(shape)
    return pl.BlockSpec(shape, lambda *_: zeros, pipeline_mode=pl.Buffered(1))


def _rope_kernel(pos_ref, invf_ref, ct_ref, st_ref):
    ang = pos_ref[...] * invf_ref[...]
    lane = lax.broadcasted_iota(jnp.int32, ang.shape, 1)
    c, s = jnp.cos(ang), jnp.sin(ang)
    in_rope = (lane >= MLA_NOPE) & (lane < MLA_NOPE + MLA_ROPE)
    first = lane < MLA_NOPE + MLA_ROPE // 2
    ct_ref[...] = jnp.where(lane < MLA_NOPE, 1.0, jnp.where(in_rope, c, 0.0))
    st_ref[...] = jnp.where(in_rope, jnp.where(first, -s, s), 0.0)


def _rope_tables(pos_f, invf_lane, tm):
    t = pos_f.shape[0]
    return pl.pallas_call(
        _rope_kernel,
        grid=(t // tm,),
        in_specs=[pl.BlockSpec((tm, 1), lambda i: (i, 0)), _resident((1, LANES))],
        out_specs=[pl.BlockSpec((tm, LANES), lambda i: (i, 0))] * 2,
        out_shape=[jax.ShapeDtypeStruct((t, LANES), F32)] * 2,
        compiler_params=_params(),
        name="rope_tables",
    )(pos_f, invf_lane)


def _inproj_kernel(x_ref, g_ref, w_ref, mla_ref, hqig_ref, hf_ref, guv_ref, gates_ref, *, nchunk):
    h = _rms(x_ref[...], g_ref[...]).astype(BF16)
    start = 0
    for out in (mla_ref, hqig_ref, hf_ref, guv_ref, gates_ref):
        width = out.shape[1]
        for c0 in range(0, width, nchunk):
            c1 = min(c0 + nchunk, width)
            out[:, c0:c1] = jnp.dot(
                h, w_ref[:, start + c0:start + c1], preferred_element_type=F32).astype(out.dtype)
        start += width


def _inproj(x2, g, w_all, tm):
    t, d = x2.shape
    widths = (SEG_MLA, SEG_HQIG, SEG_HF, SEG_GUV, SEG_GATES)
    dtypes = (BF16, BF16, F32, BF16, BF16)
    return pl.pallas_call(
        functools.partial(_inproj_kernel, nchunk=512),
        grid=(t // tm,),
        in_specs=[pl.BlockSpec((tm, d), lambda i: (i, 0)), _resident((1, d)),
                  _resident(w_all.shape)],
        out_specs=[pl.BlockSpec((tm, w), lambda i: (i, 0)) for w in widths],
        out_shape=[jax.ShapeDtypeStruct((t, w), dt) for w, dt in zip(widths, dtypes)],
        compiler_params=_params(),
        name="in_proj",
    )(x2, g, w_all)


def _mla_prep_kernel(z_ref, ct_ref, st_ref, qg_ref, kvg_ref, wq_ref, wqr_ref, wk_ref, wv_ref,
                     q_ref, k_ref, v_ref, *, scale):
    ct, st = ct_ref[...], st_ref[...]
    cq = z_ref[:, :MLA_Q_RANK].astype(F32)
    qn = _rms(cq, qg_ref[...]).astype(BF16)
    q = jnp.dot(qn, wq_ref[...], preferred_element_type=F32)
    qr = jnp.dot(qn, wqr_ref[...], preferred_element_type=F32)
    ckv = z_ref[:, MLA_Q_RANK:MLA_Q_RANK + MLA_KV_RANK].astype(F32)
    kvn = _rms(ckv, kvg_ref[...]).astype(BF16)
    kn = jnp.dot(kvn, wk_ref[...], preferred_element_type=F32)
    v = jnp.dot(kvn, wv_ref[...], preferred_element_type=F32)
    lane = lax.broadcasted_iota(jnp.int32, v.shape, 1)
    v_ref[...] = jnp.where(lane % LANES == MLA_V, 1.0, v).astype(BF16)
    o = MLA_Q_RANK + MLA_KV_RANK
    krope = z_ref[:, o:o + LANES].astype(F32) * ct + z_ref[:, o + LANES:o + 2 * LANES].astype(F32) * st
    for h in range(MLA_HEADS):
        sl = slice(h * LANES, (h + 1) * LANES)
        q_ref[:, sl] = ((q[:, sl] * ct + qr[:, sl] * st) * scale).astype(BF16)
        k_ref[:, sl] = (kn[:, sl] + krope).astype(BF16)


def _mla_prep(z_mla, ct, st, qg, kvg, wq, wqr, wk, wv, tm):
    t = z_mla.shape[0]
    hw = MLA_HEADS * LANES
    row = lambda w: pl.BlockSpec((tm, w), lambda i: (i, 0))
    return pl.pallas_call(
        functools.partial(_mla_prep_kernel, scale=float((MLA_NOPE + MLA_ROPE) ** -0.5 * math.log2(math.e))),
        grid=(t // tm,),
        in_specs=[row(SEG_MLA), row(LANES), row(LANES), _resident(qg.shape), _resident(kvg.shape),
                  _resident(wq.shape), _resident(wqr.shape), _resident(wk.shape), _resident(wv.shape)],
        out_specs=[row(hw)] * 3,
        out_shape=[jax.ShapeDtypeStruct((t, hw), BF16)] * 3,
        compiler_params=_params(),
        name="mla_prep",
    )(z_mla, ct, st, qg, kvg, wq, wqr, wk, wv)


def _attn_kernel(q_ref, k_ref, v_ref, o_ref, *scratch, tq):
    tk = tq // 2
    i = pl.program_id(2)
    per_head = len(scratch) // ATTN_HEADS_PER_STEP
    every, lo, hi = slice(0, tq), slice(0, tk), slice(tk, tq)
    causal = (lax.broadcasted_iota(jnp.int32, (tk, tk), 1)
              <= lax.broadcasted_iota(jnp.int32, (tk, tk), 0))

    def kv_rows(j):
        return pl.ds(pl.multiple_of(j * tk, tk), tk)

    class Head:
        def __init__(self, n):
            (s0, s1, p0, p1, al0, al1, self.m, self.acc) = scratch[n * per_head:(n + 1) * per_head]
            self.s, self.p, self.al = (s0, s1), (p0, p1), (al0, al1)
            self.lanes = slice(n * LANES, (n + 1) * LANES)

        def scores(self, j, slot, rows=every):
            self.s[slot][rows, :] = lax.dot_general(
                q_ref[rows, self.lanes], k_ref[kv_rows(j), self.lanes],
                (((1,), (1,)), ((), ())), preferred_element_type=F32)

        def accumulate(self, j, slot, rows=every):
            self.acc[rows, :] = self.al[slot][rows, :] * self.acc[rows, :] + jnp.dot(
                self.p[slot][rows, :], v_ref[kv_rows(j), self.lanes], preferred_element_type=F32)

        def softmax(self, slot, rows=every, masked=False):
            s = self.s[slot][rows, :]
            if masked:
                s = jnp.where(causal, s, NEG_BIG)
            m = self.m[rows, :]
            m_new = jnp.maximum(m, jnp.max(s, axis=1, keepdims=True))
            self.p[slot][rows, :] = jnp.concatenate(
                [jnp.exp2(s[:, c:c + LANES] - m_new) for c in range(0, tk, LANES)],
                axis=1).astype(BF16)
            self.al[slot][rows, :] = jnp.exp2(m - m_new)
            self.m[rows, :] = m_new

        def start(self):
            self.m[...] = jnp.full_like(self.m, NEG_BIG)
            self.acc[...] = jnp.zeros_like(self.acc)
            self.p[1][...] = jnp.zeros_like(self.p[1])
            self.al[1][...] = jnp.ones_like(self.al[1])
            self.scores(0, 0)

        def pair(self, p):
            a = 2 * p
            self.scores(a + 1, 1)
            self.accumulate(jnp.maximum(a - 1, 0), 1)
            self.softmax(0)
            self.scores(a + 2, 0)
            self.accumulate(a, 0)
            self.softmax(1)

        def finish(self):
            a = 2 * i
            self.scores(a + 1, 1, hi)
            self.accumulate(jnp.maximum(a - 1, 0), 1)
            self.softmax(0, lo, masked=True)
            self.softmax(0, hi)
            self.accumulate(a, 0)
            self.softmax(1, hi, masked=True)
            self.accumulate(a + 1, 1, hi)
            acc = self.acc[...]
            return (acc / acc[:, MLA_V:MLA_V + 1])[:, :MLA_V]

    heads = [Head(n) for n in range(ATTN_HEADS_PER_STEP)]
    for head in heads:
        head.start()

    def pairs(first, count):
        def body(t, carry):
            for head in heads:
                for n in range(count):
                    head.pair(first + count * t + n)
            return carry
        return body

    unrolled = i // ATTN_PAIRS_PER_TRIP
    lax.fori_loop(0, unrolled, pairs(0, ATTN_PAIRS_PER_TRIP), 0)
    lax.fori_loop(0, i - unrolled * ATTN_PAIRS_PER_TRIP, pairs(unrolled * ATTN_PAIRS_PER_TRIP, 1), 0)
    o_ref[...] = jnp.concatenate([head.finish() for head in heads], axis=1).astype(o_ref.dtype)


def _attention(q, k, v, batch, seq, tq):
    t = q.shape[0]
    nq = seq // tq
    tk = tq // 2
    width = ATTN_HEADS_PER_STEP * LANES
    qspec = pl.BlockSpec((tq, width), lambda b, h, i: (b * nq + i, h))
    kvspec = pl.BlockSpec((seq, width), lambda b, h, i: (b, h))
    per_head = [pltpu.VMEM((tq, tk), F32), pltpu.VMEM((tq, tk), F32),
                pltpu.VMEM((tq, tk), BF16), pltpu.VMEM((tq, tk), BF16),
                pltpu.VMEM((tq, LANES), F32), pltpu.VMEM((tq, LANES), F32),
                pltpu.VMEM((tq, LANES), F32), pltpu.VMEM((tq, LANES), F32)]
    return pl.pallas_call(
        functools.partial(_attn_kernel, tq=tq),
        grid=(batch, MLA_HEADS // ATTN_HEADS_PER_STEP, nq),
        in_specs=[qspec, kvspec, kvspec],
        out_specs=pl.BlockSpec((tq, ATTN_HEADS_PER_STEP * MLA_V), lambda b, h, i: (b * nq + i, h)),
        out_shape=jax.ShapeDtypeStruct((t, MLA_HEADS * MLA_V), BF16),
        scratch_shapes=per_head * ATTN_HEADS_PER_STEP,
        compiler_params=_params(3),
        name="mla_attention",
    )(q, k, v)


def _hgrn_kernel(lbp_ref, hn_ref, qig_ref, f_ref, o_ref, st_ref, b_scr, k_scr, q_scr, v_scr, o_scr,
                 prod_scr, rsum_scr, *, layer, n_chunks):
    C, SUB, K, HALF = HG_CHUNK, HG_SUB, HG_KEY, SUBLANES

    @pl.when(pl.program_id(1) == 0)
    def _():
        st_ref[...] = jnp.zeros_like(st_ref)

    fr = f_ref[...]
    e = jnp.exp(-jnp.abs(fr))
    inv = 1.0 / (1.0 + e)
    log_sig = jnp.minimum(fr, 0.0) + jnp.log(inv)
    sig_neg = jnp.where(fr >= 0.0, e * inv, inv)
    if layer == 0:
        log_f, kk = log_sig, sig_neg
    else:
        lbp = lbp_ref[...]
        w = jnp.exp(lbp - jnp.max(lbp, axis=0, keepdims=True))
        gamma = w / jnp.sum(w, axis=0, keepdims=True)
        lb = gamma[1:2, :]
        for j in range(2, layer + 1):
            lb = lb + gamma[j:j + 1, :]
        log_lb = jnp.log(lb)
        y = jnp.log1p(-lb) + log_sig
        log_f = jnp.maximum(log_lb, y) + jnp.log1p(jnp.exp(-jnp.abs(log_lb - y)))
        kk = (1.0 - lb) * sig_neg
    hq = qig_ref[:, 0:HG_W].astype(F32)
    qq = hq * jax.nn.sigmoid(hq)
    vv = qig_ref[:, HG_W:2 * HG_W].astype(F32)
    log_f = log_f * LOG2E
    p0 = log_f.astype(BF16)
    r1 = log_f - p0.astype(F32)
    p1 = r1.astype(BF16)
    p2 = (r1 - p1.astype(F32)).astype(BF16)
    tri = (lax.broadcasted_iota(jnp.int32, (C, C), 1)
           <= lax.broadcasted_iota(jnp.int32, (C, C), 0)).astype(BF16)
    for c in range(n_chunks):
        rs = slice(c * C, (c + 1) * C)
        b_scr[c] = (jnp.dot(tri, p0[rs], preferred_element_type=F32)
                    + jnp.dot(tri, p1[rs], preferred_element_type=F32)
                    + jnp.dot(tri, p2[rs], preferred_element_type=F32))
        k_scr[c], q_scr[c], v_scr[c] = kk[rs], qq[rs], vv[rs]

    ones = jnp.ones((K, K), BF16)
    t_idx = lax.broadcasted_iota(jnp.int32, (HALF, K), 0)
    keep = [t_idx >= s for s in range(HALF)]

    def chunk(c, carry):
        heads = [slice(h * K, (h + 1) * K) for h in range(HG_HEADS)]
        blocks = range(0, C, SUB)
        bs = [b_scr[c, :, hs] for hs in heads]
        qs = [q_scr[c, :, hs] for hs in heads]
        ks = [k_scr[c, :, hs] for hs in heads]
        vs = [qig_ref[pl.ds(pl.multiple_of(c * C, C), C), HG_W + h * K:HG_W + (h + 1) * K]
              for h in range(HG_HEADS)]

        o_state = []
        for h, hs in enumerate(heads):
            state = st_ref[h]
            o_state.append(lax.dot_general(
                (qs[h] * jnp.exp2(bs[h])).astype(BF16), state.astype(BF16),
                (((1,), (1,)), ((), ())), preferred_element_type=F32))
            blast = bs[h][C - 1:C, :]
            kdec = (ks[h] * jnp.exp2(blast - bs[h])).astype(BF16)
            st_ref[h] = state * jnp.exp2(blast) + jnp.dot(
                v_scr[c, :, hs].T.astype(BF16), kdec, preferred_element_type=F32)

        a_off = {}
        for h in range(HG_HEADS):
            for r in blocks[1:]:
                bref = bs[h][r - 1:r, :]
                qd = (qs[h][r:r + SUB] * jnp.exp2(bs[h][r:r + SUB] - bref)).astype(BF16)
                kd = (ks[h][0:r] * jnp.exp2(bref - bs[h][0:r])).astype(BF16)
                a_off[h, r] = lax.dot_general(qd, kd, (((1,), (1,)), ((), ())),
                                              preferred_element_type=F32)

        owner = []
        for h, hs in enumerate(heads):
            for r in blocks:
                prods = []
                for s in range(SUB):
                    b_row, k_row = b_scr[c, r + s:r + s + 1, hs], k_scr[c, r + s:r + s + 1, hs]
                    for half in range(s // HALF, SUB // HALF):
                        t0 = r + half * HALF
                        arg = bs[h][t0:t0 + HALF] - b_row
                        if half == s // HALF:
                            arg = jnp.where(keep[s % HALF], arg, -jnp.inf)
                        prods.append(qs[h][t0:t0 + HALF] * jnp.exp2(arg) * k_row)
                        owner.append((h, t0, r + s))
                n0 = (len(owner) - len(prods)) * HALF
                prod_scr[n0:n0 + len(prods) * HALF, :] = jnp.concatenate(prods, axis=0).astype(BF16)
        rsum_scr[...] = jnp.dot(prod_scr[...], ones, preferred_element_type=F32)

        o_half = {}
        for h in range(HG_HEADS):
            for r in blocks:
                o_i = o_state[h][r:r + SUB]
                if r > 0:
                    o_i = o_i + jnp.dot(a_off[h, r].astype(BF16), vs[h][0:r],
                                        preferred_element_type=F32)
                for half in range(SUB // HALF):
                    o_half[h, r + half * HALF] = o_i[half * HALF:(half + 1) * HALF]
        for n, (h, t0, s_row) in enumerate(owner):
            o_half[h, t0] = o_half[h, t0] + (rsum_scr[n * HALF:(n + 1) * HALF, :]
                                             * v_scr[c, s_row:s_row + 1, heads[h]])
        for h, hs in enumerate(heads):
            o_scr[c, :, hs] = jnp.concatenate([o_half[h, t0] for t0 in range(0, C, HALF)], axis=0)
        return carry

    lax.fori_loop(0, n_chunks, chunk, 0)

    hn = hn_ref[...]
    for c in range(n_chunks):
        rs = slice(c * C, (c + 1) * C)
        for h in range(HG_HEADS):
            hs = slice(h * K, (h + 1) * K)
            g_h = qig_ref[rs, 2 * HG_W + h * K:2 * HG_W + (h + 1) * K].astype(F32)
            o_ref[rs, hs] = (_rms(o_scr[c, :, hs], hn)
                             * (g_h * jax.nn.sigmoid(g_h))).astype(o_ref.dtype)


def _hgrn(lbp, hn, z_hqig, z_hf, batch, seq, layer, tt):
    t = z_hf.shape[0]
    nt = seq // tt
    row = lambda w: pl.BlockSpec((tt, w), lambda b, i: (b * nt + i, 0))
    tile = pltpu.VMEM((tt // HG_CHUNK, HG_CHUNK, HG_W), F32)
    halves = HG_SUB // SUBLANES
    pair_rows = HG_HEADS * (HG_CHUNK // HG_SUB) * (halves * (halves + 1) // 2 * SUBLANES) * SUBLANES
    return pl.pallas_call(
        functools.partial(_hgrn_kernel, layer=layer, n_chunks=tt // HG_CHUNK),
        grid=(batch, nt),
        in_specs=[_resident(lbp.shape), _resident(hn.shape), row(SEG_HQIG), row(SEG_HF)],
        out_specs=row(HG_W),
        out_shape=jax.ShapeDtypeStruct((t, HG_W), BF16),
        scratch_shapes=[pltpu.VMEM((HG_HEADS, HG_KEY, HG_KEY), F32), tile, tile, tile, tile, tile,
                        pltpu.VMEM((pair_rows, HG_KEY), BF16), pltpu.VMEM((pair_rows, HG_KEY), F32)],
        compiler_params=_params(2),
        name="hgrn2",
    )(lbp, hn, z_hqig, z_hf)


def _gmlp_kernel(z_ref, g_ref, b_ref, ws_ref, bias_ref, o_ref, *, n_chunks):
    z = z_ref[...].astype(F32)
    uv = 0.5 * z * (1.0 + lax.erf(z * (2.0 ** -0.5)))
    u, v = uv[:, :GM_W], uv[:, GM_W:]
    mu = jnp.mean(v, axis=-1, keepdims=True)
    var = jnp.mean(jnp.square(v - mu), axis=-1, keepdims=True)
    vn = ((v - mu) * lax.rsqrt(var + EPS) * g_ref[...] + b_ref[...]).astype(BF16)
    T = GM_CHUNK
    tri = lax.broadcasted_iota(jnp.int32, (T, T), 1) <= lax.broadcasted_iota(jnp.int32, (T, T), 0)
    for g in range(GM_GROUPS):
        gs = slice(g * LANES, (g + 1) * LANES)
        w = jnp.where(tri, ws_ref[g], 0.0).astype(BF16)
        bias = bias_ref[:, gs]
        for n in range(n_chunks):
            rs = slice(n * T, (n + 1) * T)
            mixed = jnp.dot(w, vn[rs, gs], preferred_element_type=F32) + bias
            o_ref[rs, gs] = (u[rs, gs] * mixed).astype(o_ref.dtype)


def _gmlp(z_guv, ln_g, ln_b, w_s, bias_full, tg):
    t = z_guv.shape[0]
    return pl.pallas_call(
        functools.partial(_gmlp_kernel, n_chunks=tg // GM_CHUNK),
        grid=(t // tg,),
        in_specs=[pl.BlockSpec((tg, SEG_GUV), lambda i: (i, 0)), _resident(ln_g.shape),
                  _resident(ln_b.shape), _resident(w_s.shape), _resident(bias_full.shape)],
        out_specs=pl.BlockSpec((tg, GM_W), lambda i: (i, 0)),
        out_shape=jax.ShapeDtypeStruct((t, GM_W), BF16),
        compiler_params=_params(),
        name="gmlp",
    )(z_guv, ln_g, ln_b, w_s, bias_full)


def _merge_kernel(x_ref, oa_ref, ob_ref, oc_ref, gates_ref, wa_ref, wb_ref, wc_ref, wm_ref, o_ref):
    d = x_ref.shape[1]
    m = None
    for idx, (br, w) in enumerate(((oa_ref, wa_ref), (ob_ref, wb_ref), (oc_ref, wc_ref))):
        y = jnp.dot(br[...], w[...], preferred_element_type=F32)
        term = jax.nn.sigmoid(gates_ref[:, idx * d:(idx + 1) * d].astype(F32)) * y
        m = term if m is None else m + term
    o_ref[...] = x_ref[...] + jnp.dot(m.astype(BF16), wm_ref[...], preferred_element_type=F32)


def _merge(x2, oa, ob, oc, gates, wa, wb, wc, wm, tm):
    t, d = x2.shape
    row = lambda w: pl.BlockSpec((tm, w), lambda i: (i, 0))
    return pl.pallas_call(
        _merge_kernel,
        grid=(t // tm,),
        in_specs=[row(d), row(oa.shape[1]), row(ob.shape[1]), row(oc.shape[1]), row(SEG_GATES),
                  _resident(wa.shape), _resident(wb.shape), _resident(wc.shape), _resident(wm.shape)],
        out_specs=row(d),
        out_shape=jax.ShapeDtypeStruct((t, d), F32),
        compiler_params=_params(),
        name="merge",
    )(x2, oa, ob, oc, gates, wa, wb, wc, wm)


def _memkv_kernel(m_ref, g_ref, w_ref, o_ref):
    h = _rms(m_ref[...], g_ref[...]).astype(BF16)
    o_ref[...] = jnp.dot(h, w_ref[...], preferred_element_type=F32).astype(o_ref.dtype)


def _memkv(mem2, g, w_kv):
    n, d = mem2.shape
    return pl.pallas_call(
        _memkv_kernel,
        grid=(1,),
        in_specs=[_resident(mem2.shape), _resident(g.shape), _resident(w_kv.shape)],
        out_specs=pl.BlockSpec((n, 2 * d), lambda i: (0, 0)),
        out_shape=jax.ShapeDtypeStruct((n, 2 * d), BF16),
        compiler_params=_params(),
        name="mem_kv",
    )(mem2, g, w_kv)


def _xattn_kernel(x_ref, g_ref, wq_ref, kv_ref, wo_ref, o_ref, *, scale):
    x = x_ref[...]
    d = x.shape[1]
    hd = d // X_HEADS
    h = _rms(x, g_ref[...]).astype(BF16)
    q = (jnp.dot(h, wq_ref[...], preferred_element_type=F32) * scale).astype(BF16)
    outs = []
    for hh in range(X_HEADS):
        k = kv_ref[:, hh * hd:(hh + 1) * hd]
        v = kv_ref[:, d + hh * hd:d + (hh + 1) * hd]
        s = lax.dot_general(q[:, hh * hd:(hh + 1) * hd], k, (((1,), (1,)), ((), ())),
                            preferred_element_type=F32)
        p = jnp.exp(s - jnp.max(s, axis=-1, keepdims=True))
        p = p / jnp.sum(p, axis=-1, keepdims=True)
        outs.append(jnp.dot(p.astype(BF16), v, preferred_element_type=F32).astype(BF16))
    o = jnp.concatenate(outs, axis=-1)
    o_ref[...] = x + jnp.dot(o, wo_ref[...], preferred_element_type=F32)


def _xattn(x2, g, wq, kv, wo, seq, n_mem, tm):
    t, d = x2.shape
    per_batch = seq // tm
    row = pl.BlockSpec((tm, d), lambda i: (i, 0))
    return pl.pallas_call(
        functools.partial(_xattn_kernel, scale=float((d // X_HEADS) ** -0.5)),
        grid=(t // tm,),
        in_specs=[row, _resident(g.shape), _resident(wq.shape),
                  pl.BlockSpec((n_mem, 2 * d), lambda i: (i // per_batch, 0)), _resident(wo.shape)],
        out_specs=row,
        out_shape=jax.ShapeDtypeStruct((t, d), F32),
        compiler_params=_params(),
        name="cross_attn",
    )(x2, g, wq, kv, wo)


def _ffn_kernel(x_ref, g_ref, wi_ref, wo_ref, fg_ref, o_ref, *, d_ff, n_split, final):
    x = x_ref[...]
    h = _rms(x, g_ref[...]).astype(BF16)
    acc = x
    tiles = d_ff // MXU_TILE
    edges = [(c * tiles // n_split) * MXU_TILE for c in range(n_split + 1)]
    for c0, c1 in zip(edges[:-1], edges[1:]):
        gt = jnp.dot(h, wi_ref[:, c0:c1], preferred_element_type=F32)
        up = jnp.dot(h, wi_ref[:, d_ff + c0:d_ff + c1], preferred_element_type=F32)
        a = (gt * jax.nn.sigmoid(gt) * up).astype(BF16)
        acc = acc + jnp.dot(a, wo_ref[c0:c1, :], preferred_element_type=F32)
    o_ref[...] = _rms(acc, fg_ref[...]) if final else acc


def _ffn(x2, g, w_in, w_out, final_g, final, tm):
    t, d = x2.shape
    d_ff = w_out.shape[0]
    row = pl.BlockSpec((tm, d), lambda i: (i, 0))
    return pl.pallas_call(
        functools.partial(_ffn_kernel, d_ff=d_ff, n_split=2, final=final),
        grid=(t // tm,),
        in_specs=[row, _resident(g.shape), _resident(w_in.shape), _resident(w_out.shape),
                  _resident(final_g.shape)],
        out_specs=row,
        out_shape=jax.ShapeDtypeStruct((t, d), F32),
        compiler_params=_params(),
        name="ffn",
    )(x2, g, w_in, w_out, final_g)


def _pad_heads(w, heads, width):
    r = w.shape[0]
    w = w.reshape(r, heads, width)
    return jnp.pad(w, ((0, 0), (0, 0), (0, LANES - width))).reshape(r, heads * LANES)


def _rope_slab(w_rope):
    half = MLA_ROPE // 2
    pad = ((0, 0), (0, 0), (MLA_NOPE, LANES - MLA_NOPE - MLA_ROPE))
    swapped = jnp.concatenate([w_rope[..., half:], w_rope[..., :half]], axis=-1)
    return jnp.pad(w_rope, pad), jnp.pad(swapped, pad)


def _layer_weights(l, w_in, mla_w_uq, mla_w_ukv):
    w = w_in[l].astype(BF16)
    d = w.shape[0]
    o = [0]
    for s in (MLA_Q_RANK, MLA_KV_RANK, MLA_ROPE, HG_W, HG_W, HG_W, HG_W, 2 * GM_W, 3 * d):
        o.append(o[-1] + s)
    c_q, c_kv, k_r, hq, hf, hi, hg, guv, gates = (w[:, o[i]:o[i + 1]] for i in range(9))
    kr, kr_rot = _rope_slab(k_r.reshape(d, 1, MLA_ROPE))
    w_all = jnp.concatenate(
        [c_q, c_kv, kr.reshape(d, LANES), kr_rot.reshape(d, LANES), hq, hi, hg, hf, guv, gates],
        axis=1)

    uq = mla_w_uq[l].astype(BF16).reshape(MLA_Q_RANK, MLA_HEADS, MLA_NOPE + MLA_ROPE)
    rope, rope_rot = _rope_slab(uq[..., MLA_NOPE:])
    wq = (jnp.pad(uq[..., :MLA_NOPE], ((0, 0), (0, 0), (0, LANES - MLA_NOPE))) + rope)
    hw = MLA_HEADS * LANES
    wq, wqr = wq.reshape(MLA_Q_RANK, hw), rope_rot.reshape(MLA_Q_RANK, hw)
    ukv = mla_w_ukv[l].astype(BF16).reshape(MLA_KV_RANK, MLA_HEADS, MLA_NOPE + MLA_V)
    wk = _pad_heads(ukv[..., :MLA_NOPE].reshape(MLA_KV_RANK, -1), MLA_HEADS, MLA_NOPE)
    wv = _pad_heads(ukv[..., MLA_NOPE:].reshape(MLA_KV_RANK, -1), MLA_HEADS, MLA_V)
    return w_all, wq, wqr, wk, wv


def kernel(x, mem, positions, mix_norm, w_in, mla_q_norm, mla_w_uq, mla_kv_norm, mla_w_ukv,
           hg_lower_bounds, hg_head_norm, gm_ln_g, gm_ln_b, gm_w_s, gm_b_s,
           w_branch_a, w_branch_b, w_branch_c, w_mix_out,
           xa_norm, mem_norm, xa_w_q, xa_w_kv, xa_w_o,
           ffn_norm, ffn_w_in, ffn_w_out, final_norm):
    batch, seq, d = x.shape
    depth = w_in.shape[0]
    n_mem = mem.shape[1]
    t = batch * seq
    x2 = x.reshape(t, d)
    mem2 = mem.reshape(batch * n_mem, d)
    vec = lambda a: a.reshape(1, -1).astype(F32)

    tm = min(512, seq)
    inv_freq = ROPE_BASE ** (-jnp.arange(0, MLA_ROPE, 2, dtype=F32) / MLA_ROPE)
    invf_lane = jnp.zeros((LANES,), F32).at[MLA_NOPE:MLA_NOPE + MLA_ROPE].set(
        jnp.concatenate([inv_freq, inv_freq])).reshape(1, LANES)
    ct, st = _rope_tables(positions.astype(F32).reshape(t, 1), invf_lane, tm)

    for l in range(depth):
        w_all, wq, wqr, wk, wv = _layer_weights(l, w_in, mla_w_uq, mla_w_ukv)
        z_mla, z_hqig, z_hf, z_guv, z_gates = _inproj(x2, vec(mix_norm[l]), w_all, tm)
        q, k, v = _mla_prep(z_mla, ct, st, vec(mla_q_norm[l]), vec(mla_kv_norm[l]), wq, wqr, wk, wv, tm)
        oa = _attention(q, k, v, batch, seq, tm)
        ob = _hgrn(hg_lower_bounds.astype(F32), vec(hg_head_norm[l]), z_hqig, z_hf, batch, seq, l, tm)
        bias_full = jnp.repeat(gm_b_s[l].T, GM_W // GM_GROUPS, axis=1)
        oc = _gmlp(z_guv, vec(gm_ln_g[l]), vec(gm_ln_b[l]), gm_w_s[l], bias_full, tm)
        x2 = _merge(x2, oa, ob, oc, z_gates, w_branch_a[l].astype(BF16), w_branch_b[l].astype(BF16),
                    w_branch_c[l].astype(BF16), w_mix_out[l].astype(BF16), tm)
        kv = _memkv(mem2, vec(mem_norm[l]), xa_w_kv[l].astype(BF16))
        x2 = _xattn(x2, vec(xa_norm[l]), xa_w_q[l].astype(BF16), kv, xa_w_o[l].astype(BF16),
                    seq, n_mem, tm)
        x2 = _ffn(x2, vec(ffn_norm[l]), ffn_w_in[l].astype(BF16), ffn_w_out[l].astype(BF16),
                  vec(final_norm), l == depth - 1, tm)
    return x2.reshape(batch, seq, d)
```

```python
import functools
import math

import jax
import jax.numpy as jnp
from jax import lax
from jax.experimental import pallas as pl
from jax.experimental.pallas import tpu as pltpu

F32 = jnp.float32
BF16 = jnp.bfloat16

EPS = 1e-6
LANES = 128
SUBLANES = 8
MXU_TILE = 256
LOG2E = math.log2(math.e)
VMEM_LIMIT = 56 * 1024 * 1024

MLA_HEADS = 8
MLA_Q_RANK = 384
MLA_KV_RANK = 256
MLA_NOPE = 64
MLA_ROPE = 32
MLA_V = 64
ROPE_BASE = 10000.0
HG_HEADS = 4
HG_KEY = 128
HG_CHUNK = 64
HG_SUB = 16
HG_W = HG_HEADS * HG_KEY
GM_GROUPS = 4
GM_CHUNK = 128
GM_W = 512
X_HEADS = 4
NEG_BIG = -1e30
ATTN_HEADS_PER_STEP = 2
ATTN_PAIRS_PER_TRIP = 2

SEG_MLA = MLA_Q_RANK + MLA_KV_RANK + 2 * LANES
SEG_HQIG = 3 * HG_W
SEG_HF = HG_W
SEG_GUV = 2 * GM_W
SEG_GATES = 3 * 1024


def _params(n_axes=1):
    return pltpu.CompilerParams(
        dimension_semantics=("arbitrary",) * n_axes, vmem_limit_bytes=VMEM_LIMIT)


def _rms(x, g):
    return x * lax.rsqrt(jnp.mean(x * x, axis=-1, keepdims=True) + EPS) * g


def _resident(shape):
    zeros = (0,) * len(shape)
    return pl.BlockSpec(shape, lambda *_: zeros, pipeline_mode=pl.Buffered(1))


def _rope_kernel(pos_ref, invf_ref, ct_ref, st_ref):
    ang = pos_ref[...] * invf_ref[...]
    lane = lax.broadcasted_iota(jnp.int32, ang.shape, 1)
    c, s = jnp.cos(ang), jnp.sin(ang)
    in_rope = (lane >= MLA_NOPE) & (lane < MLA_NOPE + MLA_ROPE)
    first = lane < MLA_NOPE + MLA_ROPE // 2
    ct_ref[...] = jnp.where(lane < MLA_NOPE, 1.0, jnp.where(in_rope, c, 0.0))
    st_ref[...] = jnp.where(in_rope, jnp.where(first, -s, s), 0.0)


def _rope_tables(pos_f, invf_lane, tm):
    t = pos_f.shape[0]
    return pl.pallas_call(
        _rope_kernel,
        grid=(t // tm,),
        in_specs=[pl.BlockSpec((tm, 1), lambda i: (i, 0)), _resident((1, LANES))],
        out_specs=[pl.BlockSpec((tm, LANES), lambda i: (i, 0))] * 2,
        out_shape=[jax.ShapeDtypeStruct((t, LANES), F32)] * 2,
        compiler_params=_params(),
        name="rope_tables",
    )(pos_f, invf_lane)


def _inproj_kernel(x_ref, g_ref, wm_ref, wh_ref, wguv_ref, wgates_ref,
                   mla_ref, hqig_ref, hf_ref, guv_ref, gates_ref, *, nchunk):
    h = _rms(x_ref[...], g_ref[...]).astype(BF16)

    def project(w_ref, c0, c1):
        return jnp.dot(h, w_ref[:, c0:c1], preferred_element_type=F32)

    for c, (out, o0) in enumerate(((hqig_ref, 0), (hf_ref, 0), (hqig_ref, HG_W), (hqig_ref, 2 * HG_W))):
        out[:, o0:o0 + HG_W] = project(wh_ref, c * HG_W, (c + 1) * HG_W).astype(out.dtype)
    for out, w_ref in ((mla_ref, wm_ref), (guv_ref, wguv_ref), (gates_ref, wgates_ref)):
        width = out.shape[1]
        for c0 in range(0, width, nchunk):
            c1 = min(c0 + nchunk, width)
            out[:, c0:c1] = project(w_ref, c0, c1).astype(out.dtype)


def _inproj(x2, g, weights, tm):
    t, d = x2.shape
    widths = (SEG_MLA, SEG_HQIG, SEG_HF, SEG_GUV, SEG_GATES)
    dtypes = (BF16, BF16, F32, BF16, BF16)
    return pl.pallas_call(
        functools.partial(_inproj_kernel, nchunk=512),
        grid=(t // tm,),
        in_specs=[pl.BlockSpec((tm, d), lambda i: (i, 0)), _resident((1, d))]
                 + [_resident(w.shape) for w in weights],
        out_specs=[pl.BlockSpec((tm, w), lambda i: (i, 0)) for w in widths],
        out_shape=[jax.ShapeDtypeStruct((t, w), dt) for w, dt in zip(widths, dtypes)],
        compiler_params=_params(),
        name="in_proj",
    )(x2, g, *weights)


def _mla_prep_kernel(z_ref, ct_ref, st_ref, qg_ref, kvg_ref, wq_ref, wk_ref, wv_ref,
                     q_ref, k_ref, v_ref, *, scale):
    ct, st = ct_ref[...], st_ref[...]
    cq = z_ref[:, :MLA_Q_RANK].astype(F32)
    qn = _rms(cq, qg_ref[...]).astype(BF16)
    q = jnp.dot(qn, wq_ref[...], preferred_element_type=F32)
    ckv = z_ref[:, MLA_Q_RANK:MLA_Q_RANK + MLA_KV_RANK].astype(F32)
    kvn = _rms(ckv, kvg_ref[...]).astype(BF16)
    kn = jnp.dot(kvn, wk_ref[...], preferred_element_type=F32)
    v = jnp.dot(kvn, wv_ref[...], preferred_element_type=F32)
    lane = lax.broadcasted_iota(jnp.int32, v.shape, 1)
    v_ref[...] = jnp.where(lane % LANES == MLA_V, 1.0, v).astype(BF16)
    o = MLA_Q_RANK + MLA_KV_RANK
    krope = z_ref[:, o:o + LANES].astype(F32) * ct + z_ref[:, o + LANES:o + 2 * LANES].astype(F32) * st
    for h in range(MLA_HEADS):
        sl = slice(h * LANES, (h + 1) * LANES)
        q_h = q[:, sl]
        swapped = pltpu.roll(q_h, LANES - MLA_ROPE, axis=1)
        q_ref[:, sl] = ((q_h * ct + swapped * st) * scale).astype(BF16)
        k_ref[:, sl] = (kn[:, sl] + krope).astype(BF16)


def _mla_prep(z_mla, ct, st, qg, kvg, wq, wk, wv, tm):
    t = z_mla.shape[0]
    hw = MLA_HEADS * LANES
    row = lambda w: pl.BlockSpec((tm, w), lambda i: (i, 0))
    return pl.pallas_call(
        functools.partial(_mla_prep_kernel, scale=float((MLA_NOPE + MLA_ROPE) ** -0.5 * math.log2(math.e))),
        grid=(t // tm,),
        in_specs=[row(SEG_MLA), row(LANES), row(LANES), _resident(qg.shape), _resident(kvg.shape),
                  _resident(wq.shape), _resident(wk.shape), _resident(wv.shape)],
        out_specs=[row(hw)] * 3,
        out_shape=[jax.ShapeDtypeStruct((t, hw), BF16)] * 3,
        compiler_params=_params(),
        name="mla_prep",
    )(z_mla, ct, st, qg, kvg, wq, wk, wv)


def _attn_kernel(q_ref, k_ref, v_ref, o_ref, *scratch, tq):
    tk = tq // 2
    i = pl.program_id(2)
    per_head = len(scratch) // ATTN_HEADS_PER_STEP
    every, lo, hi = slice(0, tq), slice(0, tk), slice(tk, tq)
    causal = (lax.broadcasted_iota(jnp.int32, (tk, tk), 1)
              <= lax.broadcasted_iota(jnp.int32, (tk, tk), 0))

    def kv_rows(j):
        return pl.ds(pl.multiple_of(j * tk, tk), tk)

    class Head:
        def __init__(self, n):
            (s0, s1, p0, p1, al0, al1, self.m, self.acc) = scratch[n * per_head:(n + 1) * per_head]
            self.s, self.p, self.al = (s0, s1), (p0, p1), (al0, al1)
            self.lanes = slice(n * LANES, (n + 1) * LANES)

        def scores(self, j, slot, rows=every):
            self.s[slot][rows, :] = lax.dot_general(
                q_ref[rows, self.lanes], k_ref[kv_rows(j), self.lanes],
                (((1,), (1,)), ((), ())), preferred_element_type=F32)

        def accumulate(self, j, slot, rows=every):
            self.acc[rows, :] = self.al[slot][rows, :] * self.acc[rows, :] + jnp.dot(
                self.p[slot][rows, :], v_ref[kv_rows(j), self.lanes], preferred_element_type=F32)

        def softmax(self, slot, rows=every, masked=False):
            s = self.s[slot][rows, :]
            if masked:
                s = jnp.where(causal, s, NEG_BIG)
            m = self.m[rows, :]
            m_new = jnp.maximum(m, jnp.max(s, axis=1, keepdims=True))
            self.p[slot][rows, :] = jnp.concatenate(
                [jnp.exp2(s[:, c:c + LANES] - m_new) for c in range(0, tk, LANES)],
                axis=1).astype(BF16)
            self.al[slot][rows, :] = jnp.exp2(m - m_new)
            self.m[rows, :] = m_new

        def start(self):
            self.m[...] = jnp.full_like(self.m, NEG_BIG)
            self.acc[...] = jnp.zeros_like(self.acc)
            self.p[1][...] = jnp.zeros_like(self.p[1])
            self.al[1][...] = jnp.ones_like(self.al[1])
            self.scores(0, 0)

        def pair(self, p):
            a = 2 * p
            self.scores(a + 1, 1)
            self.accumulate(jnp.maximum(a - 1, 0), 1)
            self.softmax(0)
            self.scores(a + 2, 0)
            self.accumulate(a, 0)
            self.softmax(1)

        def finish(self):
            a = 2 * i
            self.scores(a + 1, 1, hi)
            self.accumulate(jnp.maximum(a - 1, 0), 1)
            self.softmax(0, lo, masked=True)
            self.softmax(0, hi)
            self.accumulate(a, 0)
            self.softmax(1, hi, masked=True)
            self.accumulate(a + 1, 1, hi)
            acc = self.acc[...]
            return (acc / acc[:, MLA_V:MLA_V + 1])[:, :MLA_V]

    heads = [Head(n) for n in range(ATTN_HEADS_PER_STEP)]
    for head in heads:
        head.start()

    def pairs(first, count):
        def body(t, carry):
            for head in heads:
                for n in range(count):
                    head.pair(first + count * t + n)
            return carry
        return body

    unrolled = i // ATTN_PAIRS_PER_TRIP
    lax.fori_loop(0, unrolled, pairs(0, ATTN_PAIRS_PER_TRIP), 0)
    lax.fori_loop(0, i - unrolled * ATTN_PAIRS_PER_TRIP, pairs(unrolled * ATTN_PAIRS_PER_TRIP, 1), 0)
    o_ref[...] = jnp.concatenate([head.finish() for head in heads], axis=1).astype(o_ref.dtype)


def _attention(q, k, v, batch, seq, tq):
    t = q.shape[0]
    nq = seq // tq
    tk = tq // 2
    width = ATTN_HEADS_PER_STEP * LANES
    qspec = pl.BlockSpec((tq, width), lambda b, h, i: (b * nq + i, h))
    kvspec = pl.BlockSpec((seq, width), lambda b, h, i: (b, h))
    per_head = [pltpu.VMEM((tq, tk), F32), pltpu.VMEM((tq, tk), F32),
                pltpu.VMEM((tq, tk), BF16), pltpu.VMEM((tq, tk), BF16),
                pltpu.VMEM((tq, LANES), F32), pltpu.VMEM((tq, LANES), F32),
                pltpu.VMEM((tq, LANES), F32), pltpu.VMEM((tq, LANES), F32)]
    return pl.pallas_call(
        functools.partial(_attn_kernel, tq=tq),
        grid=(batch, MLA_HEADS // ATTN_HEADS_PER_STEP, nq),
        in_specs=[qspec, kvspec, kvspec],
        out_specs=pl.BlockSpec((tq, ATTN_HEADS_PER_STEP * MLA_V), lambda b, h, i: (b * nq + i, h)),
        out_shape=jax.ShapeDtypeStruct((t, MLA_HEADS * MLA_V), BF16),
        scratch_shapes=per_head * ATTN_HEADS_PER_STEP,
        compiler_params=_params(3),
        name="mla_attention",
    )(q, k, v)


def _hgrn_kernel(lbp_ref, hn_ref, qig_ref, f_ref, o_ref, st_ref, b_scr, k_scr, q_scr, v_scr, o_scr,
                 prod_scr, rsum_scr, *, layer, n_chunks):
    C, SUB, K, HALF = HG_CHUNK, HG_SUB, HG_KEY, SUBLANES

    @pl.when(pl.program_id(1) == 0)
    def _():
        st_ref[...] = jnp.zeros_like(st_ref)

    fr = f_ref[...]
    e = jnp.exp(-jnp.abs(fr))
    inv = 1.0 / (1.0 + e)
    log_sig = jnp.minimum(fr, 0.0) + jnp.log(inv)
    sig_neg = jnp.where(fr >= 0.0, e * inv, inv)
    if layer == 0:
        log_f, kk = log_sig, sig_neg
    else:
        lbp = lbp_ref[...]
        w = jnp.exp(lbp - jnp.max(lbp, axis=0, keepdims=True))
        gamma = w / jnp.sum(w, axis=0, keepdims=True)
        lb = gamma[1:2, :]
        for j in range(2, layer + 1):
            lb = lb + gamma[j:j + 1, :]
        log_lb = jnp.log(lb)
        y = jnp.log1p(-lb) + log_sig
        log_f = jnp.maximum(log_lb, y) + jnp.log1p(jnp.exp(-jnp.abs(log_lb - y)))
        kk = (1.0 - lb) * sig_neg
    hq = qig_ref[:, 0:HG_W].astype(F32)
    qq = hq * jax.nn.sigmoid(hq)
    vv = qig_ref[:, HG_W:2 * HG_W].astype(F32)
    log_f = log_f * LOG2E
    p0 = log_f.astype(BF16)
    r1 = log_f - p0.astype(F32)
    p1 = r1.astype(BF16)
    p2 = (r1 - p1.astype(F32)).astype(BF16)
    tri = (lax.broadcasted_iota(jnp.int32, (C, C), 1)
           <= lax.broadcasted_iota(jnp.int32, (C, C), 0)).astype(BF16)
    for c in range(n_chunks):
        rs = slice(c * C, (c + 1) * C)
        b_scr[c] = (jnp.dot(tri, p0[rs], preferred_element_type=F32)
                    + jnp.dot(tri, p1[rs], preferred_element_type=F32)
                    + jnp.dot(tri, p2[rs], preferred_element_type=F32))
        k_scr[c], q_scr[c], v_scr[c] = kk[rs], qq[rs], vv[rs]

    ones = jnp.ones((K, K), BF16)
    t_idx = lax.broadcasted_iota(jnp.int32, (HALF, K), 0)
    keep = [t_idx >= s for s in range(HALF)]

    def chunk(c, carry):
        heads = [slice(h * K, (h + 1) * K) for h in range(HG_HEADS)]
        blocks = range(0, C, SUB)
        bs = [b_scr[c, :, hs] for hs in heads]
        qs = [q_scr[c, :, hs] for hs in heads]
        ks = [k_scr[c, :, hs] for hs in heads]
        vs = [qig_ref[pl.ds(pl.multiple_of(c * C, C), C), HG_W + h * K:HG_W + (h + 1) * K]
              for h in range(HG_HEADS)]

        o_state = []
        for h, hs in enumerate(heads):
            state = st_ref[h]
            o_state.append(lax.dot_general(
                (qs[h] * jnp.exp2(bs[h])).astype(BF16), state.astype(BF16),
                (((1,), (1,)), ((), ())), preferred_element_type=F32))
            blast = bs[h][C - 1:C, :]
            kdec = (ks[h] * jnp.exp2(blast - bs[h])).astype(BF16)
            st_ref[h] = state * jnp.exp2(blast) + jnp.dot(
                v_scr[c, :, hs].T.astype(BF16), kdec, preferred_element_type=F32)

        a_off = {}
        for h in range(HG_HEADS):
            for r in blocks[1:]:
                bref = bs[h][r - 1:r, :]
                qd = (qs[h][r:r + SUB] * jnp.exp2(bs[h][r:r + SUB] - bref)).astype(BF16)
                kd = (ks[h][0:r] * jnp.exp2(bref - bs[h][0:r])).astype(BF16)
                a_off[h, r] = lax.dot_general(qd, kd, (((1,), (1,)), ((), ())),
                                              preferred_element_type=F32)

        owner = []
        for h, hs in enumerate(heads):
            for r in blocks:
                prods = []
                for s in range(SUB):
                    b_row, k_row = b_scr[c, r + s:r + s + 1, hs], k_scr[c, r + s:r + s + 1, hs]
                    for half in range(s // HALF, SUB // HALF):
                        t0 = r + half * HALF
                        arg = bs[h][t0:t0 + HALF] - b_row
                        if half == s // HALF:
                            arg = jnp.where(keep[s % HALF], arg, -jnp.inf)
                        prods.append(qs[h][t0:t0 + HALF] * jnp.exp2(arg) * k_row)
                        owner.append((h, t0, r + s))
                n0 = (len(owner) - len(prods)) * HALF
                prod_scr[n0:n0 + len(prods) * HALF, :] = jnp.concatenate(prods, axis=0).astype(BF16)
        rsum_scr[...] = jnp.dot(prod_scr[...], ones, preferred_element_type=F32)

        o_half = {}
        for h in range(HG_HEADS):
            for r in blocks:
                o_i = o_state[h][r:r + SUB]
                if r > 0:
                    o_i = o_i + jnp.dot(a_off[h, r].astype(BF16), vs[h][0:r],
                                        preferred_element_type=F32)
                for half in range(SUB // HALF):
                    o_half[h, r + half * HALF] = o_i[half * HALF:(half + 1) * HALF]
        for n, (h, t0, s_row) in enumerate(owner):
            o_half[h, t0] = o_half[h, t0] + (rsum_scr[n * HALF:(n + 1) * HALF, :]
                                             * v_scr[c, s_row:s_row + 1, heads[h]])
        for h, hs in enumerate(heads):
            o_scr[c, :, hs] = jnp.concatenate([o_half[h, t0] for t0 in range(0, C, HALF)], axis=0)
        return carry

    lax.fori_loop(0, n_chunks, chunk, 0)

    hn = hn_ref[...]
    for c in range(n_chunks):
        rs = slice(c * C, (c + 1) * C)
        for h in range(HG_HEADS):
            hs = slice(h * K, (h + 1) * K)
            g_h = qig_ref[rs, 2 * HG_W + h * K:2 * HG_W + (h + 1) * K].astype(F32)
            o_ref[rs, hs] = (_rms(o_scr[c, :, hs], hn)
                             * (g_h * jax.nn.sigmoid(g_h))).astype(o_ref.dtype)


def _hgrn(lbp, hn, z_hqig, z_hf, batch, seq, layer, tt):
    t = z_hf.shape[0]
    nt = seq // tt
    row = lambda w: pl.BlockSpec((tt, w), lambda b, i: (b * nt + i, 0))
    tile = pltpu.VMEM((tt // HG_CHUNK, HG_CHUNK, HG_W), F32)
    halves = HG_SUB // SUBLANES
    pair_rows = HG_HEADS * (HG_CHUNK // HG_SUB) * (halves * (halves + 1) // 2 * SUBLANES) * SUBLANES
    return pl.pallas_call(
        functools.partial(_hgrn_kernel, layer=layer, n_chunks=tt // HG_CHUNK),
        grid=(batch, nt),
        in_specs=[_resident(lbp.shape), _resident(hn.shape), row(SEG_HQIG), row(SEG_HF)],
        out_specs=row(HG_W),
        out_shape=jax.ShapeDtypeStruct((t, HG_W), BF16),
        scratch_shapes=[pltpu.VMEM((HG_HEADS, HG_KEY, HG_KEY), F32), tile, tile, tile, tile, tile,
                        pltpu.VMEM((pair_rows, HG_KEY), BF16), pltpu.VMEM((pair_rows, HG_KEY), F32)],
        compiler_params=_params(2),
        name="hgrn2",
    )(lbp, hn, z_hqig, z_hf)


def _gmlp_kernel(z_ref, g_ref, b_ref, ws_ref, bias_ref, o_ref, *, n_chunks):
    z = z_ref[...].astype(F32)
    uv = 0.5 * z * (1.0 + lax.erf(z * (2.0 ** -0.5)))
    u, v = uv[:, :GM_W], uv[:, GM_W:]
    mu = jnp.mean(v, axis=-1, keepdims=True)
    var = jnp.mean(jnp.square(v - mu), axis=-1, keepdims=True)
    vn = ((v - mu) * lax.rsqrt(var + EPS) * g_ref[...] + b_ref[...]).astype(BF16)
    T = GM_CHUNK
    tri = lax.broadcasted_iota(jnp.int32, (T, T), 1) <= lax.broadcasted_iota(jnp.int32, (T, T), 0)
    for g in range(GM_GROUPS):
        gs = slice(g * LANES, (g + 1) * LANES)
        w = jnp.where(tri, ws_ref[g], 0.0).astype(BF16)
        bias = bias_ref[:, gs]
        for n in range(n_chunks):
            rs = slice(n * T, (n + 1) * T)
            mixed = jnp.dot(w, vn[rs, gs], preferred_element_type=F32) + bias
            o_ref[rs, gs] = (u[rs, gs] * mixed).astype(o_ref.dtype)


def _gmlp(z_guv, ln_g, ln_b, w_s, bias_full, tg):
    t = z_guv.shape[0]
    return pl.pallas_call(
        functools.partial(_gmlp_kernel, n_chunks=tg // GM_CHUNK),
        grid=(t // tg,),
        in_specs=[pl.BlockSpec((tg, SEG_GUV), lambda i: (i, 0)), _resident(ln_g.shape),
                  _resident(ln_b.shape), _resident(w_s.shape), _resident(bias_full.shape)],
        out_specs=pl.BlockSpec((tg, GM_W), lambda i: (i, 0)),
        out_shape=jax.ShapeDtypeStruct((t, GM_W), BF16),
        compiler_params=_params(),
        name="gmlp",
    )(z_guv, ln_g, ln_b, w_s, bias_full)


def _memkv_kernel(m_ref, g_ref, w_ref, o_ref):
    h = _rms(m_ref[...], g_ref[...]).astype(BF16)
    o_ref[...] = jnp.dot(h, w_ref[...], preferred_element_type=F32).astype(o_ref.dtype)


def _memkv(mem2, g, w_kv):
    n, d = mem2.shape
    return pl.pallas_call(
        _memkv_kernel,
        grid=(1,),
        in_specs=[_resident(mem2.shape), _resident(g.shape), _resident(w_kv.shape)],
        out_specs=pl.BlockSpec((n, 2 * d), lambda i: (0, 0)),
        out_shape=jax.ShapeDtypeStruct((n, 2 * d), BF16),
        compiler_params=_params(),
        name="mem_kv",
    )(mem2, g, w_kv)


def _merge_xattn_kernel(x_ref, oa_ref, ob_ref, oc_ref, gates_ref, wa_ref, wb_ref, wc_ref, wm_ref,
                        g_ref, wq_ref, kv_ref, wo_ref, o_ref, *, scale):
    d = x_ref.shape[1]
    m = None
    for idx, (br, w) in enumerate(((oa_ref, wa_ref), (ob_ref, wb_ref), (oc_ref, wc_ref))):
        y = jnp.dot(br[...], w[...], preferred_element_type=F32)
        term = jax.nn.sigmoid(gates_ref[:, idx * d:(idx + 1) * d].astype(F32)) * y
        m = term if m is None else m + term
    x = x_ref[...] + jnp.dot(m.astype(BF16), wm_ref[...], preferred_element_type=F32)

    hd = d // X_HEADS
    h = _rms(x, g_ref[...]).astype(BF16)
    q = (jnp.dot(h, wq_ref[...], preferred_element_type=F32) * scale).astype(BF16)
    outs = []
    for hh in range(X_HEADS):
        k = kv_ref[:, hh * hd:(hh + 1) * hd]
        v = kv_ref[:, d + hh * hd:d + (hh + 1) * hd]
        s = lax.dot_general(q[:, hh * hd:(hh + 1) * hd], k, (((1,), (1,)), ((), ())),
                            preferred_element_type=F32)
        p = jnp.exp(s - jnp.max(s, axis=-1, keepdims=True))
        p = p / jnp.sum(p, axis=-1, keepdims=True)
        outs.append(jnp.dot(p.astype(BF16), v, preferred_element_type=F32).astype(BF16))
    o = jnp.concatenate(outs, axis=-1)
    o_ref[...] = x + jnp.dot(o, wo_ref[...], preferred_element_type=F32)


def _merge_xattn(x2, oa, ob, oc, gates, merge_weights, g, wq, kv, wo, seq, n_mem, tm):
    t, d = x2.shape
    per_batch = seq // tm
    row = lambda w: pl.BlockSpec((tm, w), lambda i: (i, 0))
    return pl.pallas_call(
        functools.partial(_merge_xattn_kernel, scale=float((d // X_HEADS) ** -0.5)),
        grid=(t // tm,),
        in_specs=[row(d), row(oa.shape[1]), row(ob.shape[1]), row(oc.shape[1]), row(SEG_GATES)]
                 + [_resident(w.shape) for w in merge_weights]
                 + [_resident(g.shape), _resident(wq.shape),
                    pl.BlockSpec((n_mem, 2 * d), lambda i: (i // per_batch, 0)), _resident(wo.shape)],
        out_specs=row(d),
        out_shape=jax.ShapeDtypeStruct((t, d), F32),
        compiler_params=_params(),
        name="merge_cross_attn",
    )(x2, oa, ob, oc, gates, *merge_weights, g, wq, kv, wo)


def _ffn_kernel(x_ref, g_ref, wi_ref, wo_ref, fg_ref, o_ref, *, d_ff, n_split, final):
    x = x_ref[...]
    h = _rms(x, g_ref[...]).astype(BF16)
    acc = x
    tiles = d_ff // MXU_TILE
    edges = [(c * tiles // n_split) * MXU_TILE for c in range(n_split + 1)]
    for c0, c1 in zip(edges[:-1], edges[1:]):
        gt = jnp.dot(h, wi_ref[:, c0:c1], preferred_element_type=F32)
        up = jnp.dot(h, wi_ref[:, d_ff + c0:d_ff + c1], preferred_element_type=F32)
        a = (gt * jax.nn.sigmoid(gt) * up).astype(BF16)
        acc = acc + jnp.dot(a, wo_ref[c0:c1, :], preferred_element_type=F32)
    o_ref[...] = _rms(acc, fg_ref[...]) if final else acc


def _ffn(x2, g, w_in, w_out, final_g, final, tm):
    t, d = x2.shape
    d_ff = w_out.shape[0]
    row = pl.BlockSpec((tm, d), lambda i: (i, 0))
    return pl.pallas_call(
        functools.partial(_ffn_kernel, d_ff=d_ff, n_split=2, final=final),
        grid=(t // tm,),
        in_specs=[row, _resident(g.shape), _resident(w_in.shape), _resident(w_out.shape),
                  _resident(final_g.shape)],
        out_specs=row,
        out_shape=jax.ShapeDtypeStruct((t, d), F32),
        compiler_params=_params(),
        name="ffn",
    )(x2, g, w_in, w_out, final_g)


def _pad_heads(w, heads, width):
    r = w.shape[0]
    w = w.reshape(r, heads, width)
    return jnp.pad(w, ((0, 0), (0, 0), (0, LANES - width))).reshape(r, heads * LANES)


def _rope_slab(w_rope):
    half = MLA_ROPE // 2
    pad = ((0, 0), (0, 0), (MLA_NOPE, LANES - MLA_NOPE - MLA_ROPE))
    swapped = jnp.concatenate([w_rope[..., half:], w_rope[..., :half]], axis=-1)
    return jnp.pad(w_rope, pad), jnp.pad(swapped, pad)


def _layer_weights(l, w_in, mla_w_uq, mla_w_ukv):
    w = w_in[l]
    d = w.shape[0]
    o = [0]
    for s in (MLA_Q_RANK + MLA_KV_RANK, MLA_ROPE, 4 * HG_W, 2 * GM_W, 3 * d):
        o.append(o[-1] + s)
    latents, k_r, hgrn, guv, gates = (w[:, o[i]:o[i + 1]].astype(BF16) for i in range(5))
    kr, kr_rot = _rope_slab(k_r.reshape(d, 1, MLA_ROPE))
    w_mla = jnp.concatenate([latents, kr.reshape(d, LANES), kr_rot.reshape(d, LANES)], axis=1)

    uq = mla_w_uq[l].astype(BF16).reshape(MLA_Q_RANK, MLA_HEADS, MLA_NOPE + MLA_ROPE)
    half = MLA_ROPE // 2
    rope = uq[..., MLA_NOPE:]
    wq = jnp.concatenate([uq, rope[..., half:], rope[..., :half]], axis=-1)
    hw = MLA_HEADS * LANES
    wq = wq.reshape(MLA_Q_RANK, hw)
    ukv = mla_w_ukv[l].astype(BF16).reshape(MLA_KV_RANK, MLA_HEADS, MLA_NOPE + MLA_V)
    wk = _pad_heads(ukv[..., :MLA_NOPE].reshape(MLA_KV_RANK, -1), MLA_HEADS, MLA_NOPE)
    wv = _pad_heads(ukv[..., MLA_NOPE:].reshape(MLA_KV_RANK, -1), MLA_HEADS, MLA_V)
    return (w_mla, hgrn, guv, gates), (wq, wk, wv)


def kernel(x, mem, positions, mix_norm, w_in, mla_q_norm, mla_w_uq, mla_kv_norm, mla_w_ukv,
           hg_lower_bounds, hg_head_norm, gm_ln_g, gm_ln_b, gm_w_s, gm_b_s,
           w_branch_a, w_branch_b, w_branch_c, w_mix_out,
           xa_norm, mem_norm, xa_w_q, xa_w_kv, xa_w_o,
           ffn_norm, ffn_w_in, ffn_w_out, final_norm):
    batch, seq, d = x.shape
    depth = w_in.shape[0]
    n_mem = mem.shape[1]
    t = batch * seq
    x2 = x.reshape(t, d)
    mem2 = mem.reshape(batch * n_mem, d)
    vec = lambda a: a.reshape(1, -1).astype(F32)

    tm = min(512, seq)
    inv_freq = ROPE_BASE ** (-jnp.arange(0, MLA_ROPE, 2, dtype=F32) / MLA_ROPE)
    invf_lane = jnp.zeros((LANES,), F32).at[MLA_NOPE:MLA_NOPE + MLA_ROPE].set(
        jnp.concatenate([inv_freq, inv_freq])).reshape(1, LANES)
    ct, st = _rope_tables(positions.astype(F32).reshape(t, 1), invf_lane, tm)

    for l in range(depth):
        in_weights, mla_weights = _layer_weights(l, w_in, mla_w_uq, mla_w_ukv)
        z_mla, z_hqig, z_hf, z_guv, z_gates = _inproj(x2, vec(mix_norm[l]), in_weights, tm)
        q, k, v = _mla_prep(z_mla, ct, st, vec(mla_q_norm[l]), vec(mla_kv_norm[l]), *mla_weights, tm)
        oa = _attention(q, k, v, batch, seq, tm)
        ob = _hgrn(hg_lower_bounds.astype(F32), vec(hg_head_norm[l]), z_hqig, z_hf, batch, seq, l, tm)
        bias_full = jnp.repeat(gm_b_s[l].T, GM_W // GM_GROUPS, axis=1)
        oc = _gmlp(z_guv, vec(gm_ln_g[l]), vec(gm_ln_b[l]), gm_w_s[l], bias_full, tm)
        kv = _memkv(mem2, vec(mem_norm[l]), xa_w_kv[l].astype(BF16))
        merge_weights = [w[l].astype(BF16) for w in (w_branch_a, w_branch_b, w_branch_c, w_mix_out)]
        x2 = _merge_xattn(x2, oa, ob, oc, z_gates, merge_weights, vec(xa_norm[l]),
                          xa_w_q[l].astype(BF16), kv, xa_w_o[l].astype(BF16), seq, n_mem, tm)
        x2 = _ffn(x2, vec(ffn_norm[l]), ffn_w_in[l].astype(BF16), ffn_w_out[l].astype(BF16),
                  vec(final_norm), l == depth - 1, tm)
    return x2.reshape(batch, seq, d)
```

```python
import functools
import math

import jax
import jax.numpy as jnp
from jax import lax
from jax.experimental import pallas as pl
from jax.experimental.pallas import tpu as pltpu

F32 = jnp.float32
BF16 = jnp.bfloat16

EPS = 1e-6
LANES = 128
SUBLANES = 8
MXU_TILE = 256
LOG2E = math.log2(math.e)
VMEM_LIMIT = 56 * 1024 * 1024

MLA_HEADS = 8
MLA_Q_RANK = 384
MLA_KV_RANK = 256
MLA_NOPE = 64
MLA_ROPE = 32
MLA_V = 64
ROPE_BASE = 10000.0
HG_HEADS = 4
HG_KEY = 128
HG_CHUNK = 64
HG_SUB = 16
HG_W = HG_HEADS * HG_KEY
GM_GROUPS = 4
GM_CHUNK = 128
GM_W = 512
X_HEADS = 4
NEG_BIG = -1e30
ATTN_HEADS_PER_STEP = 2
ATTN_PAIRS_PER_TRIP = 2
ATTN_TQ = 512
ATTN_TK = ATTN_TQ // 2

SEG_MLA = MLA_Q_RANK + MLA_KV_RANK + 2 * LANES
SEG_HQIG = 3 * HG_W
SEG_HF = HG_W
SEG_GUV = 2 * GM_W
SEG_GATES = 3 * 1024


def _params(n_axes=1):
    return pltpu.CompilerParams(
        dimension_semantics=("arbitrary",) * n_axes, vmem_limit_bytes=VMEM_LIMIT)


def _rms(x, g):
    return x * lax.rsqrt(jnp.mean(x * x, axis=-1, keepdims=True) + EPS) * g


def _resident(shape):
    zeros = (0,) * len(shape)
    return pl.BlockSpec(shape, lambda *_: zeros, pipeline_mode=pl.Buffered(1))


def _rope_kernel(pos_ref, invf_ref, ct_ref, st_ref):
    ang = pos_ref[...] * invf_ref[...]
    lane = lax.broadcasted_iota(jnp.int32, ang.shape, 1)
    c, s = jnp.cos(ang), jnp.sin(ang)
    in_rope = (lane >= MLA_NOPE) & (lane < MLA_NOPE + MLA_ROPE)
    first = lane < MLA_NOPE + MLA_ROPE // 2
    ct_ref[...] = jnp.where(lane < MLA_NOPE, 1.0, jnp.where(in_rope, c, 0.0))
    st_ref[...] = jnp.where(in_rope, jnp.where(first, -s, s), 0.0)


def _rope_tables(pos_f, invf_lane, tm):
    t = pos_f.shape[0]
    return pl.pallas_call(
        _rope_kernel,
        grid=(t // tm,),
        in_specs=[pl.BlockSpec((tm, 1), lambda i: (i, 0)), _resident((1, LANES))],
        out_specs=[pl.BlockSpec((tm, LANES), lambda i: (i, 0))] * 2,
        out_shape=[jax.ShapeDtypeStruct((t, LANES), F32)] * 2,
        compiler_params=_params(),
        name="rope_tables",
    )(pos_f, invf_lane)


def _inproj_kernel(x_ref, g_ref, wm_ref, wh_ref, wguv_ref, wgates_ref,
                   mla_ref, hqig_ref, hf_ref, guv_ref, gates_ref, *, nchunk):
    h = _rms(x_ref[...], g_ref[...]).astype(BF16)

    def project(w_ref, c0, c1):
        return jnp.dot(h, w_ref[:, c0:c1], preferred_element_type=F32)

    for c, (out, o0) in enumerate(((hqig_ref, 0), (hf_ref, 0), (hqig_ref, HG_W), (hqig_ref, 2 * HG_W))):
        out[:, o0:o0 + HG_W] = project(wh_ref, c * HG_W, (c + 1) * HG_W).astype(out.dtype)
    for out, w_ref in ((mla_ref, wm_ref), (guv_ref, wguv_ref), (gates_ref, wgates_ref)):
        width = out.shape[1]
        for c0 in range(0, width, nchunk):
            c1 = min(c0 + nchunk, width)
            out[:, c0:c1] = project(w_ref, c0, c1).astype(out.dtype)


def _inproj(x2, g, weights, tm):
    t, d = x2.shape
    widths = (SEG_MLA, SEG_HQIG, SEG_HF, SEG_GUV, SEG_GATES)
    dtypes = (BF16, BF16, F32, BF16, BF16)
    return pl.pallas_call(
        functools.partial(_inproj_kernel, nchunk=512),
        grid=(t // tm,),
        in_specs=[pl.BlockSpec((tm, d), lambda i: (i, 0)), _resident((1, d))]
                 + [_resident(w.shape) for w in weights],
        out_specs=[pl.BlockSpec((tm, w), lambda i: (i, 0)) for w in widths],
        out_shape=[jax.ShapeDtypeStruct((t, w), dt) for w, dt in zip(widths, dtypes)],
        compiler_params=_params(),
        name="in_proj",
    )(x2, g, *weights)


def _mla_prep_kernel(z_ref, ct_ref, st_ref, qg_ref, kvg_ref, wq_ref, wk_ref, wv_ref,
                     q_ref, k_ref, v_ref, *, scale):
    ct, st = ct_ref[...], st_ref[...]
    cq = z_ref[:, :MLA_Q_RANK].astype(F32)
    qn = _rms(cq, qg_ref[...]).astype(BF16)
    q = jnp.dot(qn, wq_ref[...], preferred_element_type=F32)
    ckv = z_ref[:, MLA_Q_RANK:MLA_Q_RANK + MLA_KV_RANK].astype(F32)
    kvn = _rms(ckv, kvg_ref[...]).astype(BF16)
    kn = jnp.dot(kvn, wk_ref[...], preferred_element_type=F32)
    vt = lax.dot_general(wv_ref[...], kvn, (((1,), (1,)), ((), ())), preferred_element_type=F32)
    row = lax.broadcasted_iota(jnp.int32, vt.shape, 0)
    vt = jnp.where(row % LANES == MLA_V, 1.0, vt).astype(BF16)
    for n in range(v_ref.shape[0]):
        v_ref[n] = vt[:, n * ATTN_TK:(n + 1) * ATTN_TK]
    o = MLA_Q_RANK + MLA_KV_RANK
    krope = z_ref[:, o:o + LANES].astype(F32) * ct + z_ref[:, o + LANES:o + 2 * LANES].astype(F32) * st
    for h in range(MLA_HEADS):
        sl = slice(h * LANES, (h + 1) * LANES)
        q_h = q[:, sl]
        swapped = pltpu.roll(q_h, LANES - MLA_ROPE, axis=1)
        q_ref[:, sl] = ((q_h * ct + swapped * st) * scale).astype(BF16)
        k_ref[:, sl] = (kn[:, sl] + krope).astype(BF16)


def _mla_prep(z_mla, ct, st, qg, kvg, wq, wk, wv, tm):
    t = z_mla.shape[0]
    hw = MLA_HEADS * LANES
    row = lambda w: pl.BlockSpec((tm, w), lambda i: (i, 0))
    return pl.pallas_call(
        functools.partial(_mla_prep_kernel, scale=float((MLA_NOPE + MLA_ROPE) ** -0.5 * math.log2(math.e))),
        grid=(t // tm,),
        in_specs=[row(SEG_MLA), row(LANES), row(LANES), _resident(qg.shape), _resident(kvg.shape),
                  _resident(wq.shape), _resident(wk.shape), _resident(wv.shape)],
        out_specs=[row(hw), row(hw), pl.BlockSpec((tm // ATTN_TK, hw, ATTN_TK), lambda i: (i, 0, 0))],
        out_shape=[jax.ShapeDtypeStruct((t, hw), BF16)] * 2
                  + [jax.ShapeDtypeStruct((t // ATTN_TK, hw, ATTN_TK), BF16)],
        compiler_params=_params(),
        name="mla_prep",
    )(z_mla, ct, st, qg, kvg, wq, wk, wv)


def _attn_kernel(q_ref, k_ref, vt_ref, o_ref, *scratch, tq):
    tk = tq // 2
    i = pl.program_id(2)
    per_head = len(scratch) // ATTN_HEADS_PER_STEP
    every, lo, hi = slice(0, tq), slice(0, tk), slice(tk, tq)
    causal = (lax.broadcasted_iota(jnp.int32, (tk, tk), 0)
              <= lax.broadcasted_iota(jnp.int32, (tk, tk), 1))

    def kv_rows(j):
        return pl.ds(pl.multiple_of(j * tk, tk), tk)

    class Head:
        def __init__(self, n):
            (s0, s1, p0, p1, al0, al1, self.m, self.acc) = scratch[n * per_head:(n + 1) * per_head]
            self.s, self.p, self.al = (s0, s1), (p0, p1), (al0, al1)
            self.lanes = slice(n * LANES, (n + 1) * LANES)

        def scores(self, j, slot, cols=every):
            self.s[slot][:, cols] = lax.dot_general(
                k_ref[kv_rows(j), self.lanes], q_ref[cols, self.lanes],
                (((1,), (1,)), ((), ())), preferred_element_type=F32)

        def accumulate(self, j, slot, cols=every):
            self.acc[:, cols] = self.al[slot][:, cols] * self.acc[:, cols] + jnp.dot(
                vt_ref[j, self.lanes, :], self.p[slot][:, cols], preferred_element_type=F32)

        def softmax(self, slot, cols=every, masked=False):
            s = self.s[slot][:, cols]
            if masked:
                s = jnp.where(causal, s, NEG_BIG)
            m = self.m[:, cols]
            m_new = jnp.maximum(m, jnp.max(s, axis=0, keepdims=True))
            self.p[slot][:, cols] = jnp.exp2(s - m_new).astype(BF16)
            self.al[slot][:, cols] = jnp.exp2(m - m_new)
            self.m[:, cols] = m_new

        def start(self):
            self.m[...] = jnp.full_like(self.m, NEG_BIG)
            self.acc[...] = jnp.zeros_like(self.acc)
            self.p[1][...] = jnp.zeros_like(self.p[1])
            self.al[1][...] = jnp.ones_like(self.al[1])
            self.scores(0, 0)

        def pair(self, p):
            a = 2 * p
            self.scores(a + 1, 1)
            self.accumulate(jnp.maximum(a - 1, 0), 1)
            self.softmax(0)
            self.scores(a + 2, 0)
            self.accumulate(a, 0)
            self.softmax(1)

        def finish(self):
            a = 2 * i
            self.scores(a + 1, 1, hi)
            self.accumulate(jnp.maximum(a - 1, 0), 1)
            self.softmax(0, lo, masked=True)
            self.softmax(0, hi)
            self.accumulate(a, 0)
            self.softmax(1, hi, masked=True)
            self.accumulate(a + 1, 1, hi)
            acc = self.acc[...]
            return (acc / acc[MLA_V:MLA_V + 1, :]).T[:, :MLA_V]

    heads = [Head(n) for n in range(ATTN_HEADS_PER_STEP)]
    for head in heads:
        head.start()

    def pairs(first, count):
        def body(t, carry):
            for head in heads:
                for n in range(count):
                    head.pair(first + count * t + n)
            return carry
        return body

    unrolled = i // ATTN_PAIRS_PER_TRIP
    lax.fori_loop(0, unrolled, pairs(0, ATTN_PAIRS_PER_TRIP), 0)
    lax.fori_loop(0, i - unrolled * ATTN_PAIRS_PER_TRIP, pairs(unrolled * ATTN_PAIRS_PER_TRIP, 1), 0)
    o_ref[...] = jnp.concatenate([head.finish() for head in heads], axis=1).astype(o_ref.dtype)


def _attention(q, k, vt, batch, seq, tq):
    t = q.shape[0]
    nq = seq // tq
    tk = tq // 2
    width = ATTN_HEADS_PER_STEP * LANES
    qspec = pl.BlockSpec((tq, width), lambda b, h, i: (b * nq + i, h))
    kspec = pl.BlockSpec((seq, width), lambda b, h, i: (b, h))
    vtspec = pl.BlockSpec((seq // tk, width, tk), lambda b, h, i: (b, h, 0))
    per_head = [pltpu.VMEM((tk, tq), F32), pltpu.VMEM((tk, tq), F32),
                pltpu.VMEM((tk, tq), BF16), pltpu.VMEM((tk, tq), BF16),
                pltpu.VMEM((1, tq), F32), pltpu.VMEM((1, tq), F32),
                pltpu.VMEM((1, tq), F32), pltpu.VMEM((LANES, tq), F32)]
    return pl.pallas_call(
        functools.partial(_attn_kernel, tq=tq),
        grid=(batch, MLA_HEADS // ATTN_HEADS_PER_STEP, nq),
        in_specs=[qspec, kspec, vtspec],
        out_specs=pl.BlockSpec((tq, ATTN_HEADS_PER_STEP * MLA_V), lambda b, h, i: (b * nq + i, h)),
        out_shape=jax.ShapeDtypeStruct((t, MLA_HEADS * MLA_V), BF16),
        scratch_shapes=per_head * ATTN_HEADS_PER_STEP,
        compiler_params=_params(3),
        name="mla_attention",
    )(q, k, vt)


def _hgrn_kernel(lbp_ref, hn_ref, qig_ref, f_ref, o_ref, st_ref, b_scr, k_scr, q_scr, v_scr, o_scr,
                 prod_scr, rsum_scr, *, layer, n_chunks):
    C, SUB, K, HALF = HG_CHUNK, HG_SUB, HG_KEY, SUBLANES

    @pl.when(pl.program_id(1) == 0)
    def _():
        st_ref[...] = jnp.zeros_like(st_ref)

    fr = f_ref[...]
    e = jnp.exp(-jnp.abs(fr))
    inv = 1.0 / (1.0 + e)
    log_sig = jnp.minimum(fr, 0.0) + jnp.log(inv)
    sig_neg = jnp.where(fr >= 0.0, e * inv, inv)
    if layer == 0:
        log_f, kk = log_sig, sig_neg
    else:
        lbp = lbp_ref[...]
        w = jnp.exp(lbp - jnp.max(lbp, axis=0, keepdims=True))
        gamma = w / jnp.sum(w, axis=0, keepdims=True)
        lb = gamma[1:2, :]
        for j in range(2, layer + 1):
            lb = lb + gamma[j:j + 1, :]
        log_lb = jnp.log(lb)
        y = jnp.log1p(-lb) + log_sig
        log_f = jnp.maximum(log_lb, y) + jnp.log1p(jnp.exp(-jnp.abs(log_lb - y)))
        kk = (1.0 - lb) * sig_neg
    hq = qig_ref[:, 0:HG_W].astype(F32)
    qq = hq * jax.nn.sigmoid(hq)
    vv = qig_ref[:, HG_W:2 * HG_W].astype(F32)
    log_f = log_f * LOG2E
    p0 = log_f.astype(BF16)
    r1 = log_f - p0.astype(F32)
    p1 = r1.astype(BF16)
    p2 = (r1 - p1.astype(F32)).astype(BF16)
    tri = (lax.broadcasted_iota(jnp.int32, (C, C), 1)
           <= lax.broadcasted_iota(jnp.int32, (C, C), 0)).astype(BF16)
    for c in range(n_chunks):
        rs = slice(c * C, (c + 1) * C)
        b_scr[c] = (jnp.dot(tri, p0[rs], preferred_element_type=F32)
                    + jnp.dot(tri, p1[rs], preferred_element_type=F32)
                    + jnp.dot(tri, p2[rs], preferred_element_type=F32))
        k_scr[c], q_scr[c], v_scr[c] = kk[rs], qq[rs], vv[rs]

    ones = jnp.ones((K, K), BF16)
    t_idx = lax.broadcasted_iota(jnp.int32, (HALF, K), 0)
    keep = [t_idx >= s for s in range(HALF)]

    def chunk(c, carry):
        heads = [slice(h * K, (h + 1) * K) for h in range(HG_HEADS)]
        blocks = range(0, C, SUB)
        bs = [b_scr[c, :, hs] for hs in heads]
        qs = [q_scr[c, :, hs] for hs in heads]
        ks = [k_scr[c, :, hs] for hs in heads]
        vs = [qig_ref[pl.ds(pl.multiple_of(c * C, C), C), HG_W + h * K:HG_W + (h + 1) * K]
              for h in range(HG_HEADS)]

        o_state = []
        for h, hs in enumerate(heads):
            state = st_ref[h]
            o_state.append(lax.dot_general(
                (qs[h] * jnp.exp2(bs[h])).astype(BF16), state.astype(BF16),
                (((1,), (1,)), ((), ())), preferred_element_type=F32))
            blast = bs[h][C - 1:C, :]
            kdec = (ks[h] * jnp.exp2(blast - bs[h])).astype(BF16)
            st_ref[h] = state * jnp.exp2(blast) + jnp.dot(
                v_scr[c, :, hs].T.astype(BF16), kdec, preferred_element_type=F32)

        a_off = {}
        for h in range(HG_HEADS):
            for r in blocks[1:]:
                bref = bs[h][r - 1:r, :]
                qd = (qs[h][r:r + SUB] * jnp.exp2(bs[h][r:r + SUB] - bref)).astype(BF16)
                kd = (ks[h][0:r] * jnp.exp2(bref - bs[h][0:r])).astype(BF16)
                a_off[h, r] = lax.dot_general(qd, kd, (((1,), (1,)), ((), ())),
                                              preferred_element_type=F32)

        owner = []
        for h, hs in enumerate(heads):
            for r in blocks:
                prods = []
                for s in range(SUB):
                    b_row, k_row = b_scr[c, r + s:r + s + 1, hs], k_scr[c, r + s:r + s + 1, hs]
                    for half in range(s // HALF, SUB // HALF):
                        t0 = r + half * HALF
                        arg = bs[h][t0:t0 + HALF] - b_row
                        if half == s // HALF:
                            arg = jnp.where(keep[s % HALF], arg, -jnp.inf)
                        prods.append(qs[h][t0:t0 + HALF] * jnp.exp2(arg) * k_row)
                        owner.append((h, t0, r + s))
                n0 = (len(owner) - len(prods)) * HALF
                prod_scr[n0:n0 + len(prods) * HALF, :] = jnp.concatenate(prods, axis=0).astype(BF16)
        rsum_scr[...] = jnp.dot(prod_scr[...], ones, preferred_element_type=F32)

        o_half = {}
        for h in range(HG_HEADS):
            for r in blocks:
                o_i = o_state[h][r:r + SUB]
                if r > 0:
                    o_i = o_i + jnp.dot(a_off[h, r].astype(BF16), vs[h][0:r],
                                        preferred_element_type=F32)
                for half in range(SUB // HALF):
                    o_half[h, r + half * HALF] = o_i[half * HALF:(half + 1) * HALF]
        for n, (h, t0, s_row) in enumerate(owner):
            o_half[h, t0] = o_half[h, t0] + (rsum_scr[n * HALF:(n + 1) * HALF, :]
                                             * v_scr[c, s_row:s_row + 1, heads[h]])
        for h, hs in enumerate(heads):
            o_scr[c, :, hs] = jnp.concatenate([o_half[h, t0] for t0 in range(0, C, HALF)], axis=0)
        return carry

    lax.fori_loop(0, n_chunks, chunk, 0)

    hn = hn_ref[...]
    for c in range(n_chunks):
        rs = slice(c * C, (c + 1) * C)
        for h in range(HG_HEADS):
            hs = slice(h * K, (h + 1) * K)
            g_h = qig_ref[rs, 2 * HG_W + h * K:2 * HG_W + (h + 1) * K].astype(F32)
            o_ref[rs, hs] = (_rms(o_scr[c, :, hs], hn)
                             * (g_h * jax.nn.sigmoid(g_h))).astype(o_ref.dtype)


def _hgrn(lbp, hn, z_hqig, z_hf, batch, seq, layer, tt):
    t = z_hf.shape[0]
    nt = seq // tt
    row = lambda w: pl.BlockSpec((tt, w), lambda b, i: (b * nt + i, 0))
    tile = pltpu.VMEM((tt // HG_CHUNK, HG_CHUNK, HG_W), F32)
    halves = HG_SUB // SUBLANES
    pair_rows = HG_HEADS * (HG_CHUNK // HG_SUB) * (halves * (halves + 1) // 2 * SUBLANES) * SUBLANES
    return pl.pallas_call(
        functools.partial(_hgrn_kernel, layer=layer, n_chunks=tt // HG_CHUNK),
        grid=(batch, nt),
        in_specs=[_resident(lbp.shape), _resident(hn.shape), row(SEG_HQIG), row(SEG_HF)],
        out_specs=row(HG_W),
        out_shape=jax.ShapeDtypeStruct((t, HG_W), BF16),
        scratch_shapes=[pltpu.VMEM((HG_HEADS, HG_KEY, HG_KEY), F32), tile, tile, tile, tile, tile,
                        pltpu.VMEM((pair_rows, HG_KEY), BF16), pltpu.VMEM((pair_rows, HG_KEY), F32)],
        compiler_params=_params(2),
        name="hgrn2",
    )(lbp, hn, z_hqig, z_hf)


def _gmlp_kernel(z_ref, g_ref, b_ref, ws_ref, bias_ref, o_ref, *, n_chunks):
    z = z_ref[...].astype(F32)
    uv = 0.5 * z * (1.0 + lax.erf(z * (2.0 ** -0.5)))
    u, v = uv[:, :GM_W], uv[:, GM_W:]
    mu = jnp.mean(v, axis=-1, keepdims=True)
    var = jnp.mean(jnp.square(v - mu), axis=-1, keepdims=True)
    vn = ((v - mu) * lax.rsqrt(var + EPS) * g_ref[...] + b_ref[...]).astype(BF16)
    T = GM_CHUNK
    tri = lax.broadcasted_iota(jnp.int32, (T, T), 1) <= lax.broadcasted_iota(jnp.int32, (T, T), 0)
    for g in range(GM_GROUPS):
        gs = slice(g * LANES, (g + 1) * LANES)
        w = jnp.where(tri, ws_ref[g], 0.0).astype(BF16)
        bias = bias_ref[:, gs]
        for n in range(n_chunks):
            rs = slice(n * T, (n + 1) * T)
            mixed = jnp.dot(w, vn[rs, gs], preferred_element_type=F32) + bias
            o_ref[rs, gs] = (u[rs, gs] * mixed).astype(o_ref.dtype)


def _gmlp(z_guv, ln_g, ln_b, w_s, bias_full, tg):
    t = z_guv.shape[0]
    return pl.pallas_call(
        functools.partial(_gmlp_kernel, n_chunks=tg // GM_CHUNK),
        grid=(t // tg,),
        in_specs=[pl.BlockSpec((tg, SEG_GUV), lambda i: (i, 0)), _resident(ln_g.shape),
                  _resident(ln_b.shape), _resident(w_s.shape), _resident(bias_full.shape)],
        out_specs=pl.BlockSpec((tg, GM_W), lambda i: (i, 0)),
        out_shape=jax.ShapeDtypeStruct((t, GM_W), BF16),
        compiler_params=_params(),
        name="gmlp",
    )(z_guv, ln_g, ln_b, w_s, bias_full)


def _memkv_kernel(m_ref, g_ref, w_ref, o_ref):
    h = _rms(m_ref[...], g_ref[...]).astype(BF16)
    o_ref[...] = jnp.dot(h, w_ref[...], preferred_element_type=F32).astype(o_ref.dtype)


def _memkv(mem2, g, w_kv):
    n, d = mem2.shape
    return pl.pallas_call(
        _memkv_kernel,
        grid=(1,),
        in_specs=[_resident(mem2.shape), _resident(g.shape), _resident(w_kv.shape)],
        out_specs=pl.BlockSpec((n, 2 * d), lambda i: (0, 0)),
        out_shape=jax.ShapeDtypeStruct((n, 2 * d), BF16),
        compiler_params=_params(),
        name="mem_kv",
    )(mem2, g, w_kv)


def _merge_xattn_kernel(x_ref, oa_ref, ob_ref, oc_ref, gates_ref, wa_ref, wb_ref, wc_ref, wm_ref,
                        g_ref, wq_ref, kv_ref, wo_ref, o_ref, *, scale):
    d = x_ref.shape[1]
    m = None
    for idx, (br, w) in enumerate(((oa_ref, wa_ref), (ob_ref, wb_ref), (oc_ref, wc_ref))):
        y = jnp.dot(br[...], w[...], preferred_element_type=F32)
        term = jax.nn.sigmoid(gates_ref[:, idx * d:(idx + 1) * d].astype(F32)) * y
        m = term if m is None else m + term
    x = x_ref[...] + jnp.dot(m.astype(BF16), wm_ref[...], preferred_element_type=F32)

    hd = d // X_HEADS
    h = _rms(x, g_ref[...]).astype(BF16)
    q = (jnp.dot(h, wq_ref[...], preferred_element_type=F32) * scale).astype(BF16)
    outs = []
    for hh in range(X_HEADS):
        k = kv_ref[:, hh * hd:(hh + 1) * hd]
        v = kv_ref[:, d + hh * hd:d + (hh + 1) * hd]
        s = lax.dot_general(q[:, hh * hd:(hh + 1) * hd], k, (((1,), (1,)), ((), ())),
                            preferred_element_type=F32)
        p = jnp.exp(s - jnp.max(s, axis=-1, keepdims=True))
        p = p / jnp.sum(p, axis=-1, keepdims=True)
        outs.append(jnp.dot(p.astype(BF16), v, preferred_element_type=F32).astype(BF16))
    o = jnp.concatenate(outs, axis=-1)
    o_ref[...] = x + jnp.dot(o, wo_ref[...], preferred_element_type=F32)


def _merge_xattn(x2, oa, ob, oc, gates, merge_weights, g, wq, kv, wo, seq, n_mem, tm):
    t, d = x2.shape
    per_batch = seq // tm
    row = lambda w: pl.BlockSpec((tm, w), lambda i: (i, 0))
    return pl.pallas_call(
        functools.partial(_merge_xattn_kernel, scale=float((d // X_HEADS) ** -0.5)),
        grid=(t // tm,),
        in_specs=[row(d), row(oa.shape[1]), row(ob.shape[1]), row(oc.shape[1]), row(SEG_GATES)]
                 + [_resident(w.shape) for w in merge_weights]
                 + [_resident(g.shape), _resident(wq.shape),
                    pl.BlockSpec((n_mem, 2 * d), lambda i: (i // per_batch, 0)), _resident(wo.shape)],
        out_specs=row(d),
        out_shape=jax.ShapeDtypeStruct((t, d), F32),
        compiler_params=_params(),
        name="merge_cross_attn",
    )(x2, oa, ob, oc, gates, *merge_weights, g, wq, kv, wo)


def _ffn_kernel(x_ref, g_ref, wi_ref, wo_ref, fg_ref, o_ref, *, d_ff, n_split, final):
    x = x_ref[...]
    h = _rms(x, g_ref[...]).astype(BF16)
    acc = x
    tiles = d_ff // MXU_TILE
    edges = [(c * tiles // n_split) * MXU_TILE for c in range(n_split + 1)]
    for c0, c1 in zip(edges[:-1], edges[1:]):
        gt = jnp.dot(h, wi_ref[:, c0:c1], preferred_element_type=F32)
        up = jnp.dot(h, wi_ref[:, d_ff + c0:d_ff + c1], preferred_element_type=F32)
        a = (gt * jax.nn.sigmoid(gt) * up).astype(BF16)
        acc = acc + jnp.dot(a, wo_ref[c0:c1, :], preferred_element_type=F32)
    o_ref[...] = _rms(acc, fg_ref[...]) if final else acc


def _ffn(x2, g, w_in, w_out, final_g, final, tm):
    t, d = x2.shape
    d_ff = w_out.shape[0]
    row = pl.BlockSpec((tm, d), lambda i: (i, 0))
    return pl.pallas_call(
        functools.partial(_ffn_kernel, d_ff=d_ff, n_split=2, final=final),
        grid=(t // tm,),
        in_specs=[row, _resident(g.shape), _resident(w_in.shape), _resident(w_out.shape),
                  _resident(final_g.shape)],
        out_specs=row,
        out_shape=jax.ShapeDtypeStruct((t, d), F32),
        compiler_params=_params(),
        name="ffn",
    )(x2, g, w_in, w_out, final_g)


def _pad_heads(w, heads, width):
    r = w.shape[0]
    w = w.reshape(r, heads, width)
    return jnp.pad(w, ((0, 0), (0, 0), (0, LANES - width))).reshape(r, heads * LANES)


def _rope_slab(w_rope):
    half = MLA_ROPE // 2
    pad = ((0, 0), (0, 0), (MLA_NOPE, LANES - MLA_NOPE - MLA_ROPE))
    swapped = jnp.concatenate([w_rope[..., half:], w_rope[..., :half]], axis=-1)
    return jnp.pad(w_rope, pad), jnp.pad(swapped, pad)


def _layer_weights(l, w_in, mla_w_uq, mla_w_ukv):
    w = w_in[l]
    d = w.shape[0]
    o = [0]
    for s in (MLA_Q_RANK + MLA_KV_RANK, MLA_ROPE, 4 * HG_W, 2 * GM_W, 3 * d):
        o.append(o[-1] + s)
    latents, k_r, hgrn, guv, gates = (w[:, o[i]:o[i + 1]].astype(BF16) for i in range(5))
    kr, kr_rot = _rope_slab(k_r.reshape(d, 1, MLA_ROPE))
    w_mla = jnp.concatenate([latents, kr.reshape(d, LANES), kr_rot.reshape(d, LANES)], axis=1)

    uq = mla_w_uq[l].astype(BF16).reshape(MLA_Q_RANK, MLA_HEADS, MLA_NOPE + MLA_ROPE)
    half = MLA_ROPE // 2
    rope = uq[..., MLA_NOPE:]
    wq = jnp.concatenate([uq, rope[..., half:], rope[..., :half]], axis=-1)
    hw = MLA_HEADS * LANES
    wq = wq.reshape(MLA_Q_RANK, hw)
    ukv = mla_w_ukv[l].astype(BF16).reshape(MLA_KV_RANK, MLA_HEADS, MLA_NOPE + MLA_V)
    wk = _pad_heads(ukv[..., :MLA_NOPE].reshape(MLA_KV_RANK, -1), MLA_HEADS, MLA_NOPE)
    wv = _pad_heads(ukv[..., MLA_NOPE:].reshape(MLA_KV_RANK, -1), MLA_HEADS, MLA_V).T
    return (w_mla, hgrn, guv, gates), (wq, wk, wv)


def kernel(x, mem, positions, mix_norm, w_in, mla_q_norm, mla_w_uq, mla_kv_norm, mla_w_ukv,
           hg_lower_bounds, hg_head_norm, gm_ln_g, gm_ln_b, gm_w_s, gm_b_s,
           w_branch_a, w_branch_b, w_branch_c, w_mix_out,
           xa_norm, mem_norm, xa_w_q, xa_w_kv, xa_w_o,
           ffn_norm, ffn_w_in, ffn_w_out, final_norm):
    batch, seq, d = x.shape
    depth = w_in.shape[0]
    n_mem = mem.shape[1]
    t = batch * seq
    x2 = x.reshape(t, d)
    mem2 = mem.reshape(batch * n_mem, d)
    vec = lambda a: a.reshape(1, -1).astype(F32)

    tm = min(512, seq)
    inv_freq = ROPE_BASE ** (-jnp.arange(0, MLA_ROPE, 2, dtype=F32) / MLA_ROPE)
    invf_lane = jnp.zeros((LANES,), F32).at[MLA_NOPE:MLA_NOPE + MLA_ROPE].set(
        jnp.concatenate([inv_freq, inv_freq])).reshape(1, LANES)
    ct, st = _rope_tables(positions.astype(F32).reshape(t, 1), invf_lane, tm)

    for l in range(depth):
        in_weights, mla_weights = _layer_weights(l, w_in, mla_w_uq, mla_w_ukv)
        z_mla, z_hqig, z_hf, z_guv, z_gates = _inproj(x2, vec(mix_norm[l]), in_weights, tm)
        q, k, v = _mla_prep(z_mla, ct, st, vec(mla_q_norm[l]), vec(mla_kv_norm[l]), *mla_weights, tm)
        oa = _attention(q, k, v, batch, seq, min(ATTN_TQ, seq))
        ob = _hgrn(hg_lower_bounds.astype(F32), vec(hg_head_norm[l]), z_hqig, z_hf, batch, seq, l, tm)
        bias_full = jnp.repeat(gm_b_s[l].T, GM_W // GM_GROUPS, axis=1)
        oc = _gmlp(z_guv, vec(gm_ln_g[l]), vec(gm_ln_b[l]), gm_w_s[l], bias_full, tm)
        kv = _memkv(mem2, vec(mem_norm[l]), xa_w_kv[l].astype(BF16))
        merge_weights = [w[l].astype(BF16) for w in (w_branch_a, w_branch_b, w_branch_c, w_mix_out)]
        x2 = _merge_xattn(x2, oa, ob, oc, z_gates, merge_weights, vec(xa_norm[l]),
                          xa_w_q[l].astype(BF16), kv, xa_w_o[l].astype(BF16), seq, n_mem, tm)
        x2 = _ffn(x2, vec(ffn_norm[l]), ffn_w_in[l].astype(BF16), ffn_w_out[l].astype(BF16),
                  vec(final_norm), l == depth - 1, tm)
    return x2.reshape(batch, seq, d)
```

```python
import functools
import math

import jax
import jax.numpy as jnp
from jax import lax
from jax.experimental import pallas as pl
from jax.experimental.pallas import tpu as pltpu

F32 = jnp.float32
BF16 = jnp.bfloat16

EPS = 1e-6
LANES = 128
SUBLANES = 8
MXU_TILE = 256
LOG2E = math.log2(math.e)
VMEM_LIMIT = 56 * 1024 * 1024

MLA_HEADS = 8
MLA_Q_RANK = 384
MLA_KV_RANK = 256
MLA_NOPE = 64
MLA_ROPE = 32
MLA_V = 64
ROPE_BASE = 10000.0
HG_HEADS = 4
HG_KEY = 128
HG_CHUNK = 64
HG_SUB = 16
HG_W = HG_HEADS * HG_KEY
GM_GROUPS = 4
GM_CHUNK = 128
GM_W = 512
X_HEADS = 4
NEG_BIG = -1e30
ATTN_HEADS_PER_STEP = 4
ATTN_PAIRS_PER_TRIP = 2
ATTN_TQ = 512
ATTN_TK = ATTN_TQ // 2

SEG_MLA = MLA_Q_RANK + MLA_KV_RANK + 2 * LANES
SEG_HQIG = 3 * HG_W
SEG_HF = HG_W
SEG_GUV = 2 * GM_W
SEG_GATES = 3 * 1024


def _params(n_axes=1):
    return pltpu.CompilerParams(
        dimension_semantics=("arbitrary",) * n_axes, vmem_limit_bytes=VMEM_LIMIT)


def _rms(x, g):
    return x * lax.rsqrt(jnp.mean(x * x, axis=-1, keepdims=True) + EPS) * g


def _resident(shape):
    zeros = (0,) * len(shape)
    return pl.BlockSpec(shape, lambda *_: zeros, pipeline_mode=pl.Buffered(1))


def _layer_resident(stacked, layer):
    index = (layer, 0, 0)
    return pl.BlockSpec((None,) + stacked.shape[1:], lambda *_: index, pipeline_mode=pl.Buffered(1))


def _rope_kernel(pos_ref, invf_ref, ct_ref, st_ref):
    ang = pos_ref[...] * invf_ref[...]
    lane = lax.broadcasted_iota(jnp.int32, ang.shape, 1)
    c, s = jnp.cos(ang), jnp.sin(ang)
    in_rope = (lane >= MLA_NOPE) & (lane < MLA_NOPE + MLA_ROPE)
    first = lane < MLA_NOPE + MLA_ROPE // 2
    ct_ref[...] = jnp.where(lane < MLA_NOPE, 1.0, jnp.where(in_rope, c, 0.0))
    st_ref[...] = jnp.where(in_rope, jnp.where(first, -s, s), 0.0)


def _rope_tables(pos_f, invf_lane, tm):
    t = pos_f.shape[0]
    return pl.pallas_call(
        _rope_kernel,
        grid=(t // tm,),
        in_specs=[pl.BlockSpec((tm, 1), lambda i: (i, 0)), _resident((1, LANES))],
        out_specs=[pl.BlockSpec((tm, LANES), lambda i: (i, 0))] * 2,
        out_shape=[jax.ShapeDtypeStruct((t, LANES), F32)] * 2,
        compiler_params=_params(),
        name="rope_tables",
    )(pos_f, invf_lane)


def _inproj_kernel(x_ref, g_ref, wm_ref, wh_ref, wguv_ref, wgates_ref,
                   mla_ref, hqig_ref, hf_ref, guv_ref, gates_ref, *, nchunk):
    h = _rms(x_ref[...], g_ref[...]).astype(BF16)

    def project(w_ref, c0, c1):
        return jnp.dot(h, w_ref[:, c0:c1], preferred_element_type=F32)

    for c, (out, o0) in enumerate(((hqig_ref, 0), (hf_ref, 0), (hqig_ref, HG_W), (hqig_ref, 2 * HG_W))):
        out[:, o0:o0 + HG_W] = project(wh_ref, c * HG_W, (c + 1) * HG_W).astype(out.dtype)
    for out, w_ref in ((mla_ref, wm_ref), (guv_ref, wguv_ref), (gates_ref, wgates_ref)):
        width = out.shape[1]
        for c0 in range(0, width, nchunk):
            c1 = min(c0 + nchunk, width)
            out[:, c0:c1] = project(w_ref, c0, c1).astype(out.dtype)


def _inproj(x2, g, weights, tm):
    t, d = x2.shape
    widths = (SEG_MLA, SEG_HQIG, SEG_HF, SEG_GUV, SEG_GATES)
    dtypes = (BF16, BF16, F32, BF16, BF16)
    return pl.pallas_call(
        functools.partial(_inproj_kernel, nchunk=512),
        grid=(t // tm,),
        in_specs=[pl.BlockSpec((tm, d), lambda i: (i, 0)), _resident((1, d))]
                 + [_resident(w.shape) for w in weights],
        out_specs=[pl.BlockSpec((tm, w), lambda i: (i, 0)) for w in widths],
        out_shape=[jax.ShapeDtypeStruct((t, w), dt) for w, dt in zip(widths, dtypes)],
        compiler_params=_params(),
        name="in_proj",
    )(x2, g, *weights)


def _mla_prep_kernel(z_ref, ct_ref, st_ref, qg_ref, kvg_ref, wq_ref, wk_ref, wv_ref,
                     q_ref, k_ref, v_ref, *, scale):
    ct, st = ct_ref[...], st_ref[...]
    cq = z_ref[:, :MLA_Q_RANK].astype(F32)
    qn = _rms(cq, qg_ref[...]).astype(BF16)
    q = jnp.dot(qn, wq_ref[...], preferred_element_type=F32)
    ckv = z_ref[:, MLA_Q_RANK:MLA_Q_RANK + MLA_KV_RANK].astype(F32)
    kvn = _rms(ckv, kvg_ref[...]).astype(BF16)
    kn = jnp.dot(kvn, wk_ref[...], preferred_element_type=F32)
    vt = lax.dot_general(wv_ref[...], kvn, (((1,), (1,)), ((), ())), preferred_element_type=F32)
    row = lax.broadcasted_iota(jnp.int32, vt.shape, 0)
    vt = jnp.where(row % LANES == MLA_V, 1.0, vt).astype(BF16)
    for n in range(v_ref.shape[0]):
        v_ref[n] = vt[:, n * ATTN_TK:(n + 1) * ATTN_TK]
    o = MLA_Q_RANK + MLA_KV_RANK
    krope = z_ref[:, o:o + LANES].astype(F32) * ct + z_ref[:, o + LANES:o + 2 * LANES].astype(F32) * st
    for h in range(MLA_HEADS):
        sl = slice(h * LANES, (h + 1) * LANES)
        q_h = q[:, sl]
        swapped = pltpu.roll(q_h, LANES - MLA_ROPE, axis=1)
        q_ref[:, sl] = ((q_h * ct + swapped * st) * scale).astype(BF16)
        k_ref[:, sl] = (kn[:, sl] + krope).astype(BF16)


def _mla_prep(z_mla, ct, st, qg, kvg, wq, wk, wv, tm):
    t = z_mla.shape[0]
    hw = MLA_HEADS * LANES
    row = lambda w: pl.BlockSpec((tm, w), lambda i: (i, 0))
    return pl.pallas_call(
        functools.partial(_mla_prep_kernel, scale=float((MLA_NOPE + MLA_ROPE) ** -0.5 * math.log2(math.e))),
        grid=(t // tm,),
        in_specs=[row(SEG_MLA), row(LANES), row(LANES), _resident(qg.shape), _resident(kvg.shape),
                  _resident(wq.shape), _resident(wk.shape), _resident(wv.shape)],
        out_specs=[row(hw), row(hw), pl.BlockSpec((tm // ATTN_TK, hw, ATTN_TK), lambda i: (i, 0, 0))],
        out_shape=[jax.ShapeDtypeStruct((t, hw), BF16)] * 2
                  + [jax.ShapeDtypeStruct((t // ATTN_TK, hw, ATTN_TK), BF16)],
        compiler_params=_params(),
        name="mla_prep",
    )(z_mla, ct, st, qg, kvg, wq, wk, wv)


def _attn_kernel(q_ref, k_ref, vt_ref, o_ref, *scratch, tq):
    tk = tq // 2
    i = pl.program_id(2)
    per_head = len(scratch) // ATTN_HEADS_PER_STEP
    every, lo, hi = slice(0, tq), slice(0, tk), slice(tk, tq)
    causal = (lax.broadcasted_iota(jnp.int32, (tk, tk), 0)
              <= lax.broadcasted_iota(jnp.int32, (tk, tk), 1))

    def kv_rows(j):
        return pl.ds(pl.multiple_of(j * tk, tk), tk)

    class Head:
        def __init__(self, n):
            (s0, s1, p0, p1, al0, al1, self.m, self.acc) = scratch[n * per_head:(n + 1) * per_head]
            self.s, self.p, self.al = (s0, s1), (p0, p1), (al0, al1)
            self.lanes = slice(n * LANES, (n + 1) * LANES)

        def scores(self, j, slot, cols=every):
            self.s[slot][:, cols] = lax.dot_general(
                k_ref[kv_rows(j), self.lanes], q_ref[cols, self.lanes],
                (((1,), (1,)), ((), ())), preferred_element_type=F32)

        def accumulate(self, j, slot, cols=every):
            self.acc[:, cols] = self.al[slot][:, cols] * self.acc[:, cols] + jnp.dot(
                vt_ref[j, self.lanes, :], self.p[slot][:, cols], preferred_element_type=F32)

        def softmax(self, slot, cols=every, masked=False):
            s = self.s[slot][:, cols]
            if masked:
                s = jnp.where(causal, s, NEG_BIG)
            m = self.m[:, cols]
            m_new = jnp.maximum(m, jnp.max(s, axis=0, keepdims=True))
            self.p[slot][:, cols] = jnp.exp2(s - m_new).astype(BF16)
            self.al[slot][:, cols] = jnp.exp2(m - m_new)
            self.m[:, cols] = m_new

        def start(self):
            self.m[...] = jnp.full_like(self.m, NEG_BIG)
            self.acc[...] = jnp.zeros_like(self.acc)
            self.p[1][...] = jnp.zeros_like(self.p[1])
            self.al[1][...] = jnp.ones_like(self.al[1])
            self.scores(0, 0)

        def pair(self, p):
            a = 2 * p
            self.scores(a + 1, 1)
            self.accumulate(jnp.maximum(a - 1, 0), 1)
            self.softmax(0)
            self.scores(a + 2, 0)
            self.accumulate(a, 0)
            self.softmax(1)

        def finish(self):
            a = 2 * i
            self.scores(a + 1, 1, hi)
            self.accumulate(jnp.maximum(a - 1, 0), 1)
            self.softmax(0, lo, masked=True)
            self.softmax(0, hi)
            self.accumulate(a, 0)
            self.softmax(1, hi, masked=True)
            self.accumulate(a + 1, 1, hi)
            acc = self.acc[...]
            return (acc / acc[MLA_V:MLA_V + 1, :]).T[:, :MLA_V]

    heads = [Head(n) for n in range(ATTN_HEADS_PER_STEP)]
    for head in heads:
        head.start()

    def pairs(first, count):
        def body(t, carry):
            for head in heads:
                for n in range(count):
                    head.pair(first + count * t + n)
            return carry
        return body

    unrolled = i // ATTN_PAIRS_PER_TRIP
    lax.fori_loop(0, unrolled, pairs(0, ATTN_PAIRS_PER_TRIP), 0)
    lax.fori_loop(0, i - unrolled * ATTN_PAIRS_PER_TRIP, pairs(unrolled * ATTN_PAIRS_PER_TRIP, 1), 0)
    o_ref[...] = jnp.concatenate([head.finish() for head in heads], axis=1).astype(o_ref.dtype)


def _attention(q, k, vt, batch, seq, tq):
    t = q.shape[0]
    nq = seq // tq
    tk = tq // 2
    width = ATTN_HEADS_PER_STEP * LANES
    qspec = pl.BlockSpec((tq, width), lambda b, h, i: (b * nq + i, h))
    kspec = pl.BlockSpec((seq, width), lambda b, h, i: (b, h))
    vtspec = pl.BlockSpec((seq // tk, width, tk), lambda b, h, i: (b, h, 0))
    per_head = [pltpu.VMEM((tk, tq), F32), pltpu.VMEM((tk, tq), F32),
                pltpu.VMEM((tk, tq), BF16), pltpu.VMEM((tk, tq), BF16),
                pltpu.VMEM((1, tq), F32), pltpu.VMEM((1, tq), F32),
                pltpu.VMEM((1, tq), F32), pltpu.VMEM((LANES, tq), F32)]
    return pl.pallas_call(
        functools.partial(_attn_kernel, tq=tq),
        grid=(batch, MLA_HEADS // ATTN_HEADS_PER_STEP, nq),
        in_specs=[qspec, kspec, vtspec],
        out_specs=pl.BlockSpec((tq, ATTN_HEADS_PER_STEP * MLA_V), lambda b, h, i: (b * nq + i, h)),
        out_shape=jax.ShapeDtypeStruct((t, MLA_HEADS * MLA_V), BF16),
        scratch_shapes=per_head * ATTN_HEADS_PER_STEP,
        compiler_params=_params(3),
        name="mla_attention",
    )(q, k, vt)


def _hgrn_kernel(lbp_ref, hn_ref, qig_ref, f_ref, o_ref, st_ref, b_scr, k_scr, q_scr, v_scr, o_scr,
                 prod_scr, rsum_scr, *, layer, n_chunks):
    C, SUB, K, HALF = HG_CHUNK, HG_SUB, HG_KEY, SUBLANES

    @pl.when(pl.program_id(1) == 0)
    def _():
        st_ref[...] = jnp.zeros_like(st_ref)

    fr = f_ref[...]
    e = jnp.exp(-jnp.abs(fr))
    inv = 1.0 / (1.0 + e)
    log_sig = jnp.minimum(fr, 0.0) + jnp.log(inv)
    sig_neg = jnp.where(fr >= 0.0, e * inv, inv)
    if layer == 0:
        log_f, kk = log_sig, sig_neg
    else:
        lbp = lbp_ref[...]
        w = jnp.exp(lbp - jnp.max(lbp, axis=0, keepdims=True))
        gamma = w / jnp.sum(w, axis=0, keepdims=True)
        lb = gamma[1:2, :]
        for j in range(2, layer + 1):
            lb = lb + gamma[j:j + 1, :]
        log_lb = jnp.log(lb)
        y = jnp.log1p(-lb) + log_sig
        log_f = jnp.maximum(log_lb, y) + jnp.log1p(jnp.exp(-jnp.abs(log_lb - y)))
        kk = (1.0 - lb) * sig_neg
    hq = qig_ref[:, 0:HG_W].astype(F32)
    qq = hq * jax.nn.sigmoid(hq)
    vv = qig_ref[:, HG_W:2 * HG_W].astype(F32)
    log_f = log_f * LOG2E
    p0 = log_f.astype(BF16)
    r1 = log_f - p0.astype(F32)
    p1 = r1.astype(BF16)
    p2 = (r1 - p1.astype(F32)).astype(BF16)
    tri = (lax.broadcasted_iota(jnp.int32, (C, C), 1)
           <= lax.broadcasted_iota(jnp.int32, (C, C), 0)).astype(BF16)
    for c in range(n_chunks):
        rs = slice(c * C, (c + 1) * C)
        b_scr[c] = (jnp.dot(tri, p0[rs], preferred_element_type=F32)
                    + jnp.dot(tri, p1[rs], preferred_element_type=F32)
                    + jnp.dot(tri, p2[rs], preferred_element_type=F32))
        k_scr[c], q_scr[c], v_scr[c] = kk[rs], qq[rs], vv[rs]

    ones = jnp.ones((K, K), BF16)
    t_idx = lax.broadcasted_iota(jnp.int32, (HALF, K), 0)
    keep = [t_idx >= s for s in range(HALF)]

    def chunk(c, carry):
        heads = [slice(h * K, (h + 1) * K) for h in range(HG_HEADS)]
        blocks = range(0, C, SUB)
        bs = [b_scr[c, :, hs] for hs in heads]
        qs = [q_scr[c, :, hs] for hs in heads]
        ks = [k_scr[c, :, hs] for hs in heads]
        vs = [qig_ref[pl.ds(pl.multiple_of(c * C, C), C), HG_W + h * K:HG_W + (h + 1) * K]
              for h in range(HG_HEADS)]

        o_state = []
        for h, hs in enumerate(heads):
            state = st_ref[h]
            o_state.append(lax.dot_general(
                (qs[h] * jnp.exp2(bs[h])).astype(BF16), state.astype(BF16),
                (((1,), (1,)), ((), ())), preferred_element_type=F32))
            blast = bs[h][C - 1:C, :]
            kdec = (ks[h] * jnp.exp2(blast - bs[h])).astype(BF16)
            st_ref[h] = state * jnp.exp2(blast) + jnp.dot(
                v_scr[c, :, hs].T.astype(BF16), kdec, preferred_element_type=F32)

        a_off = {}
        for h in range(HG_HEADS):
            for r in blocks[1:]:
                bref = bs[h][r - 1:r, :]
                qd = (qs[h][r:r + SUB] * jnp.exp2(bs[h][r:r + SUB] - bref)).astype(BF16)
                kd = (ks[h][0:r] * jnp.exp2(bref - bs[h][0:r])).astype(BF16)
                a_off[h, r] = lax.dot_general(qd, kd, (((1,), (1,)), ((), ())),
                                              preferred_element_type=F32)

        owner = []
        for h, hs in enumerate(heads):
            for r in blocks:
                prods = []
                for s in range(SUB):
                    b_row, k_row = b_scr[c, r + s:r + s + 1, hs], k_scr[c, r + s:r + s + 1, hs]
                    for half in range(s // HALF, SUB // HALF):
                        t0 = r + half * HALF
                        arg = bs[h][t0:t0 + HALF] - b_row
                        if half == s // HALF:
                            arg = jnp.where(keep[s % HALF], arg, -jnp.inf)
                        prods.append(qs[h][t0:t0 + HALF] * jnp.exp2(arg) * k_row)
                        owner.append((h, t0, r + s))
                n0 = (len(owner) - len(prods)) * HALF
                prod_scr[n0:n0 + len(prods) * HALF, :] = jnp.concatenate(prods, axis=0).astype(BF16)
        rsum_scr[...] = jnp.dot(prod_scr[...], ones, preferred_element_type=F32)

        o_half = {}
        for h in range(HG_HEADS):
            for r in blocks:
                o_i = o_state[h][r:r + SUB]
                if r > 0:
                    o_i = o_i + jnp.dot(a_off[h, r].astype(BF16), vs[h][0:r],
                                        preferred_element_type=F32)
                for half in range(SUB // HALF):
                    o_half[h, r + half * HALF] = o_i[half * HALF:(half + 1) * HALF]
        for n, (h, t0, s_row) in enumerate(owner):
            o_half[h, t0] = o_half[h, t0] + (rsum_scr[n * HALF:(n + 1) * HALF, :]
                                             * v_scr[c, s_row:s_row + 1, heads[h]])
        for h, hs in enumerate(heads):
            o_scr[c, :, hs] = jnp.concatenate([o_half[h, t0] for t0 in range(0, C, HALF)], axis=0)
        return carry

    lax.fori_loop(0, n_chunks, chunk, 0)

    hn = hn_ref[...]
    for c in range(n_chunks):
        rs = slice(c * C, (c + 1) * C)
        for h in range(HG_HEADS):
            hs = slice(h * K, (h + 1) * K)
            g_h = qig_ref[rs, 2 * HG_W + h * K:2 * HG_W + (h + 1) * K].astype(F32)
            o_ref[rs, hs] = (_rms(o_scr[c, :, hs], hn)
                             * (g_h * jax.nn.sigmoid(g_h))).astype(o_ref.dtype)


def _hgrn(lbp, hn, z_hqig, z_hf, batch, seq, layer, tt):
    t = z_hf.shape[0]
    nt = seq // tt
    row = lambda w: pl.BlockSpec((tt, w), lambda b, i: (b * nt + i, 0))
    tile = pltpu.VMEM((tt // HG_CHUNK, HG_CHUNK, HG_W), F32)
    halves = HG_SUB // SUBLANES
    pair_rows = HG_HEADS * (HG_CHUNK // HG_SUB) * (halves * (halves + 1) // 2 * SUBLANES) * SUBLANES
    return pl.pallas_call(
        functools.partial(_hgrn_kernel, layer=layer, n_chunks=tt // HG_CHUNK),
        grid=(batch, nt),
        in_specs=[_resident(lbp.shape), _resident(hn.shape), row(SEG_HQIG), row(SEG_HF)],
        out_specs=row(HG_W),
        out_shape=jax.ShapeDtypeStruct((t, HG_W), BF16),
        scratch_shapes=[pltpu.VMEM((HG_HEADS, HG_KEY, HG_KEY), F32), tile, tile, tile, tile, tile,
                        pltpu.VMEM((pair_rows, HG_KEY), BF16), pltpu.VMEM((pair_rows, HG_KEY), F32)],
        compiler_params=_params(2),
        name="hgrn2",
    )(lbp, hn, z_hqig, z_hf)


def _gmlp_kernel(z_ref, g_ref, b_ref, ws_ref, bias_ref, o_ref, *, n_chunks):
    z = z_ref[...].astype(F32)
    uv = 0.5 * z * (1.0 + lax.erf(z * (2.0 ** -0.5)))
    u, v = uv[:, :GM_W], uv[:, GM_W:]
    mu = jnp.mean(v, axis=-1, keepdims=True)
    var = jnp.mean(jnp.square(v - mu), axis=-1, keepdims=True)
    vn = ((v - mu) * lax.rsqrt(var + EPS) * g_ref[...] + b_ref[...]).astype(BF16)
    T = GM_CHUNK
    tri = lax.broadcasted_iota(jnp.int32, (T, T), 1) <= lax.broadcasted_iota(jnp.int32, (T, T), 0)
    for g in range(GM_GROUPS):
        gs = slice(g * LANES, (g + 1) * LANES)
        w = jnp.where(tri, ws_ref[g], 0.0).astype(BF16)
        bias = bias_ref[:, gs]
        for n in range(n_chunks):
            rs = slice(n * T, (n + 1) * T)
            mixed = jnp.dot(w, vn[rs, gs], preferred_element_type=F32) + bias
            o_ref[rs, gs] = (u[rs, gs] * mixed).astype(o_ref.dtype)


def _gmlp(z_guv, ln_g, ln_b, w_s, bias_full, tg):
    t = z_guv.shape[0]
    return pl.pallas_call(
        functools.partial(_gmlp_kernel, n_chunks=tg // GM_CHUNK),
        grid=(t // tg,),
        in_specs=[pl.BlockSpec((tg, SEG_GUV), lambda i: (i, 0)), _resident(ln_g.shape),
                  _resident(ln_b.shape), _resident(w_s.shape), _resident(bias_full.shape)],
        out_specs=pl.BlockSpec((tg, GM_W), lambda i: (i, 0)),
        out_shape=jax.ShapeDtypeStruct((t, GM_W), BF16),
        compiler_params=_params(),
        name="gmlp",
    )(z_guv, ln_g, ln_b, w_s, bias_full)


def _memkv_kernel(m_ref, g_ref, w_ref, o_ref):
    h = _rms(m_ref[...], g_ref[...]).astype(BF16)
    o_ref[...] = jnp.dot(h, w_ref[...], preferred_element_type=F32).astype(o_ref.dtype)


def _memkv(mem2, g, w_kv, layer):
    n, d = mem2.shape
    return pl.pallas_call(
        _memkv_kernel,
        grid=(1,),
        in_specs=[_resident(mem2.shape), _resident(g.shape), _layer_resident(w_kv, layer)],
        out_specs=pl.BlockSpec((n, 2 * d), lambda i: (0, 0)),
        out_shape=jax.ShapeDtypeStruct((n, 2 * d), BF16),
        compiler_params=_params(),
        name="mem_kv",
    )(mem2, g, w_kv)


def _merge_xattn_kernel(x_ref, oa_ref, ob_ref, oc_ref, gates_ref, wa_ref, wb_ref, wc_ref, wm_ref,
                        g_ref, wq_ref, kv_ref, wo_ref, o_ref, *, scale):
    d = x_ref.shape[1]
    m = None
    for idx, (br, w) in enumerate(((oa_ref, wa_ref), (ob_ref, wb_ref), (oc_ref, wc_ref))):
        y = jnp.dot(br[...], w[...], preferred_element_type=F32)
        term = jax.nn.sigmoid(gates_ref[:, idx * d:(idx + 1) * d].astype(F32)) * y
        m = term if m is None else m + term
    x = x_ref[...] + jnp.dot(m.astype(BF16), wm_ref[...], preferred_element_type=F32)

    hd = d // X_HEADS
    h = _rms(x, g_ref[...]).astype(BF16)
    q = (jnp.dot(h, wq_ref[...], preferred_element_type=F32) * scale).astype(BF16)
    outs = []
    for hh in range(X_HEADS):
        k = kv_ref[:, hh * hd:(hh + 1) * hd]
        v = kv_ref[:, d + hh * hd:d + (hh + 1) * hd]
        s = lax.dot_general(q[:, hh * hd:(hh + 1) * hd], k, (((1,), (1,)), ((), ())),
                            preferred_element_type=F32)
        p = jnp.exp(s - jnp.max(s, axis=-1, keepdims=True))
        p = p / jnp.sum(p, axis=-1, keepdims=True)
        outs.append(jnp.dot(p.astype(BF16), v, preferred_element_type=F32).astype(BF16))
    o = jnp.concatenate(outs, axis=-1)
    o_ref[...] = x + jnp.dot(o, wo_ref[...], preferred_element_type=F32)


def _merge_xattn(x2, oa, ob, oc, gates, merge_weights, g, wq, kv, wo, layer, seq, n_mem, tm):
    t, d = x2.shape
    per_batch = seq // tm
    row = lambda w: pl.BlockSpec((tm, w), lambda i: (i, 0))
    return pl.pallas_call(
        functools.partial(_merge_xattn_kernel, scale=float((d // X_HEADS) ** -0.5)),
        grid=(t // tm,),
        in_specs=[row(d), row(oa.shape[1]), row(ob.shape[1]), row(oc.shape[1]), row(SEG_GATES)]
                 + [_layer_resident(w, layer) for w in merge_weights]
                 + [_resident(g.shape), _layer_resident(wq, layer),
                    pl.BlockSpec((n_mem, 2 * d), lambda i: (i // per_batch, 0)),
                    _layer_resident(wo, layer)],
        out_specs=row(d),
        out_shape=jax.ShapeDtypeStruct((t, d), F32),
        compiler_params=_params(),
        name="merge_cross_attn",
    )(x2, oa, ob, oc, gates, *merge_weights, g, wq, kv, wo)


def _ffn_kernel(x_ref, g_ref, wi_ref, wo_ref, fg_ref, o_ref, *, d_ff, n_split, final):
    x = x_ref[...]
    h = _rms(x, g_ref[...]).astype(BF16)
    acc = x
    tiles = d_ff // MXU_TILE
    edges = [(c * tiles // n_split) * MXU_TILE for c in range(n_split + 1)]
    for c0, c1 in zip(edges[:-1], edges[1:]):
        gt = jnp.dot(h, wi_ref[:, c0:c1], preferred_element_type=F32)
        up = jnp.dot(h, wi_ref[:, d_ff + c0:d_ff + c1], preferred_element_type=F32)
        a = (gt * jax.nn.sigmoid(gt) * up).astype(BF16)
        acc = acc + jnp.dot(a, wo_ref[c0:c1, :], preferred_element_type=F32)
    o_ref[...] = _rms(acc, fg_ref[...]) if final else acc


def _ffn(x2, g, w_in, w_out, final_g, layer, tm):
    t, d = x2.shape
    d_ff = w_out.shape[1]
    final = layer == w_in.shape[0] - 1
    row = pl.BlockSpec((tm, d), lambda i: (i, 0))
    return pl.pallas_call(
        functools.partial(_ffn_kernel, d_ff=d_ff, n_split=2, final=final),
        grid=(t // tm,),
        in_specs=[row, _resident(g.shape), _layer_resident(w_in, layer), _layer_resident(w_out, layer),
                  _resident(final_g.shape)],
        out_specs=row,
        out_shape=jax.ShapeDtypeStruct((t, d), F32),
        compiler_params=_params(),
        name="ffn",
    )(x2, g, w_in, w_out, final_g)


def _pad_heads(w, heads, width):
    r = w.shape[0]
    w = w.reshape(r, heads, width)
    return jnp.pad(w, ((0, 0), (0, 0), (0, LANES - width))).reshape(r, heads * LANES)


def _rope_slab(w_rope):
    half = MLA_ROPE // 2
    pad = ((0, 0), (0, 0), (MLA_NOPE, LANES - MLA_NOPE - MLA_ROPE))
    swapped = jnp.concatenate([w_rope[..., half:], w_rope[..., :half]], axis=-1)
    return jnp.pad(w_rope, pad), jnp.pad(swapped, pad)


def _layer_weights(l, w_in, mla_w_uq, mla_w_ukv):
    w = w_in[l]
    d = w.shape[0]
    o = [0]
    for s in (MLA_Q_RANK + MLA_KV_RANK, MLA_ROPE, 4 * HG_W, 2 * GM_W, 3 * d):
        o.append(o[-1] + s)
    latents, k_r, hgrn, guv, gates = (w[:, o[i]:o[i + 1]].astype(BF16) for i in range(5))
    kr, kr_rot = _rope_slab(k_r.reshape(d, 1, MLA_ROPE))
    w_mla = jnp.concatenate([latents, kr.reshape(d, LANES), kr_rot.reshape(d, LANES)], axis=1)

    uq = mla_w_uq[l].astype(BF16).reshape(MLA_Q_RANK, MLA_HEADS, MLA_NOPE + MLA_ROPE)
    half = MLA_ROPE // 2
    rope = uq[..., MLA_NOPE:]
    wq = jnp.concatenate([uq, rope[..., half:], rope[..., :half]], axis=-1)
    hw = MLA_HEADS * LANES
    wq = wq.reshape(MLA_Q_RANK, hw)
    ukv = mla_w_ukv[l].astype(BF16).reshape(MLA_KV_RANK, MLA_HEADS, MLA_NOPE + MLA_V)
    wk = _pad_heads(ukv[..., :MLA_NOPE].reshape(MLA_KV_RANK, -1), MLA_HEADS, MLA_NOPE)
    wv = _pad_heads(ukv[..., MLA_NOPE:].reshape(MLA_KV_RANK, -1), MLA_HEADS, MLA_V).T
    return (w_mla, hgrn, guv, gates), (wq, wk, wv)


def kernel(x, mem, positions, mix_norm, w_in, mla_q_norm, mla_w_uq, mla_kv_norm, mla_w_ukv,
           hg_lower_bounds, hg_head_norm, gm_ln_g, gm_ln_b, gm_w_s, gm_b_s,
           w_branch_a, w_branch_b, w_branch_c, w_mix_out,
           xa_norm, mem_norm, xa_w_q, xa_w_kv, xa_w_o,
           ffn_norm, ffn_w_in, ffn_w_out, final_norm):
    batch, seq, d = x.shape
    depth = w_in.shape[0]
    n_mem = mem.shape[1]
    t = batch * seq
    x2 = x.reshape(t, d)
    mem2 = mem.reshape(batch * n_mem, d)
    vec = lambda a: a.reshape(1, -1).astype(F32)

    tm = min(512, seq)
    inv_freq = ROPE_BASE ** (-jnp.arange(0, MLA_ROPE, 2, dtype=F32) / MLA_ROPE)
    invf_lane = jnp.zeros((LANES,), F32).at[MLA_NOPE:MLA_NOPE + MLA_ROPE].set(
        jnp.concatenate([inv_freq, inv_freq])).reshape(1, LANES)
    ct, st = _rope_tables(positions.astype(F32).reshape(t, 1), invf_lane, tm)
    merge_b = [w.astype(BF16) for w in (w_branch_a, w_branch_b, w_branch_c, w_mix_out)]
    xa_q_b, xa_kv_b, xa_o_b = (w.astype(BF16) for w in (xa_w_q, xa_w_kv, xa_w_o))
    ffn_in_b, ffn_out_b = ffn_w_in.astype(BF16), ffn_w_out.astype(BF16)

    for l in range(depth):
        in_weights, mla_weights = _layer_weights(l, w_in, mla_w_uq, mla_w_ukv)
        z_mla, z_hqig, z_hf, z_guv, z_gates = _inproj(x2, vec(mix_norm[l]), in_weights, tm)
        q, k, v = _mla_prep(z_mla, ct, st, vec(mla_q_norm[l]), vec(mla_kv_norm[l]), *mla_weights, tm)
        oa = _attention(q, k, v, batch, seq, min(ATTN_TQ, seq))
        ob = _hgrn(hg_lower_bounds.astype(F32), vec(hg_head_norm[l]), z_hqig, z_hf, batch, seq, l, tm)
        bias_full = jnp.repeat(gm_b_s[l].T, GM_W // GM_GROUPS, axis=1)
        oc = _gmlp(z_guv, vec(gm_ln_g[l]), vec(gm_ln_b[l]), gm_w_s[l], bias_full, tm)
        kv = _memkv(mem2, vec(mem_norm[l]), xa_kv_b, l)
        x2 = _merge_xattn(x2, oa, ob, oc, z_gates, merge_b, vec(xa_norm[l]), xa_q_b, kv, xa_o_b,
                          l, seq, n_mem, tm)
        x2 = _ffn(x2, vec(ffn_norm[l]), ffn_in_b, ffn_out_b, vec(final_norm), l, tm)
    return x2.reshape(batch, seq, d)
```

```python
import functools
import math

import jax
import jax.numpy as jnp
from jax import lax
from jax.experimental import pallas as pl
from jax.experimental.pallas import tpu as pltpu

F32 = jnp.float32
BF16 = jnp.bfloat16

EPS = 1e-6
LANES = 128
SUBLANES = 8
MXU_TILE = 256
LOG2E = math.log2(math.e)
VMEM_LIMIT = 56 * 1024 * 1024

MLA_HEADS = 8
MLA_Q_RANK = 384
MLA_KV_RANK = 256
MLA_NOPE = 64
MLA_ROPE = 32
MLA_V = 64
ROPE_BASE = 10000.0
HG_HEADS = 4
HG_KEY = 128
HG_CHUNK = 64
HG_SUB = 16
HG_W = HG_HEADS * HG_KEY
GM_GROUPS = 4
GM_CHUNK = 128
GM_W = 512
X_HEADS = 4
NEG_BIG = -1e30
ATTN_HEADS_PER_STEP = 4
ATTN_PAIRS_PER_TRIP = 2
ATTN_TQ = 512
ATTN_TK = ATTN_TQ // 2
BF16_ROWS = 16
ATTN_VROWS = MLA_V + BF16_ROWS

SEG_MLA = MLA_Q_RANK + MLA_KV_RANK + 2 * LANES
SEG_HQIG = 3 * HG_W
SEG_HF = HG_W
SEG_GUV = 2 * GM_W
SEG_GATES = 3 * 1024


def _params(n_axes=1):
    return pltpu.CompilerParams(
        dimension_semantics=("arbitrary",) * n_axes, vmem_limit_bytes=VMEM_LIMIT)


def _rms(x, g):
    return x * lax.rsqrt(jnp.mean(x * x, axis=-1, keepdims=True) + EPS) * g


def _resident(shape):
    zeros = (0,) * len(shape)
    return pl.BlockSpec(shape, lambda *_: zeros, pipeline_mode=pl.Buffered(1))


def _layer_resident(stacked, layer):
    index = (layer, 0, 0)
    return pl.BlockSpec((None,) + stacked.shape[1:], lambda *_: index, pipeline_mode=pl.Buffered(1))


def _rope_kernel(pos_ref, invf_ref, ct_ref, st_ref):
    ang = pos_ref[...] * invf_ref[...]
    lane = lax.broadcasted_iota(jnp.int32, ang.shape, 1)
    c, s = jnp.cos(ang), jnp.sin(ang)
    in_rope = (lane >= MLA_NOPE) & (lane < MLA_NOPE + MLA_ROPE)
    first = lane < MLA_NOPE + MLA_ROPE // 2
    ct_ref[...] = jnp.where(lane < MLA_NOPE, 1.0, jnp.where(in_rope, c, 0.0))
    st_ref[...] = jnp.where(in_rope, jnp.where(first, -s, s), 0.0)


def _rope_tables(pos_f, invf_lane, tm):
    t = pos_f.shape[0]
    return pl.pallas_call(
        _rope_kernel,
        grid=(t // tm,),
        in_specs=[pl.BlockSpec((tm, 1), lambda i: (i, 0)), _resident((1, LANES))],
        out_specs=[pl.BlockSpec((tm, LANES), lambda i: (i, 0))] * 2,
        out_shape=[jax.ShapeDtypeStruct((t, LANES), F32)] * 2,
        compiler_params=_params(),
        name="rope_tables",
    )(pos_f, invf_lane)


def _inproj_kernel(x_ref, g_ref, wm_ref, wh_ref, wguv_ref, wgates_ref,
                   mla_ref, hqig_ref, hf_ref, guv_ref, gates_ref, *, nchunk):
    h = _rms(x_ref[...], g_ref[...]).astype(BF16)

    def project(w_ref, c0, c1):
        return jnp.dot(h, w_ref[:, c0:c1], preferred_element_type=F32)

    for c, (out, o0) in enumerate(((hqig_ref, 0), (hf_ref, 0), (hqig_ref, HG_W), (hqig_ref, 2 * HG_W))):
        out[:, o0:o0 + HG_W] = project(wh_ref, c * HG_W, (c + 1) * HG_W).astype(out.dtype)
    for out, w_ref in ((mla_ref, wm_ref), (guv_ref, wguv_ref), (gates_ref, wgates_ref)):
        width = out.shape[1]
        for c0 in range(0, width, nchunk):
            c1 = min(c0 + nchunk, width)
            out[:, c0:c1] = project(w_ref, c0, c1).astype(out.dtype)


def _inproj(x2, g, weights, tm):
    t, d = x2.shape
    widths = (SEG_MLA, SEG_HQIG, SEG_HF, SEG_GUV, SEG_GATES)
    dtypes = (BF16, BF16, F32, BF16, BF16)
    return pl.pallas_call(
        functools.partial(_inproj_kernel, nchunk=512),
        grid=(t // tm,),
        in_specs=[pl.BlockSpec((tm, d), lambda i: (i, 0)), _resident((1, d))]
                 + [_resident(w.shape) for w in weights],
        out_specs=[pl.BlockSpec((tm, w), lambda i: (i, 0)) for w in widths],
        out_shape=[jax.ShapeDtypeStruct((t, w), dt) for w, dt in zip(widths, dtypes)],
        compiler_params=_params(),
        name="in_proj",
    )(x2, g, *weights)


def _mla_prep_kernel(z_ref, ct_ref, st_ref, qg_ref, kvg_ref, wq_ref, wk_ref, wv_ref,
                     q_ref, k_ref, v_ref, *, scale):
    ct, st = ct_ref[...], st_ref[...]
    cq = z_ref[:, :MLA_Q_RANK].astype(F32)
    qn = _rms(cq, qg_ref[...]).astype(BF16)
    q = jnp.dot(qn, wq_ref[...], preferred_element_type=F32)
    ckv = z_ref[:, MLA_Q_RANK:MLA_Q_RANK + MLA_KV_RANK].astype(F32)
    kvn = _rms(ckv, kvg_ref[...]).astype(BF16)
    kn = jnp.dot(kvn, wk_ref[...], preferred_element_type=F32)
    vt = lax.dot_general(wv_ref[...], kvn, (((1,), (1,)), ((), ())), preferred_element_type=F32)
    row = lax.broadcasted_iota(jnp.int32, vt.shape, 0)
    vt = jnp.where(row % ATTN_VROWS == MLA_V, 1.0, vt).astype(BF16)
    for n in range(v_ref.shape[0]):
        v_ref[n] = vt[:, n * ATTN_TK:(n + 1) * ATTN_TK]
    o = MLA_Q_RANK + MLA_KV_RANK
    krope = z_ref[:, o:o + LANES].astype(F32) * ct + z_ref[:, o + LANES:o + 2 * LANES].astype(F32) * st
    for h in range(MLA_HEADS):
        sl = slice(h * LANES, (h + 1) * LANES)
        q_h = q[:, sl]
        swapped = pltpu.roll(q_h, LANES - MLA_ROPE, axis=1)
        q_ref[:, sl] = ((q_h * ct + swapped * st) * scale).astype(BF16)
        k_ref[:, sl] = (kn[:, sl] + krope).astype(BF16)


def _mla_prep(z_mla, ct, st, qg, kvg, wq, wk, wv, tm):
    t = z_mla.shape[0]
    hw = MLA_HEADS * LANES
    row = lambda w: pl.BlockSpec((tm, w), lambda i: (i, 0))
    return pl.pallas_call(
        functools.partial(_mla_prep_kernel, scale=float((MLA_NOPE + MLA_ROPE) ** -0.5 * math.log2(math.e))),
        grid=(t // tm,),
        in_specs=[row(SEG_MLA), row(LANES), row(LANES), _resident(qg.shape), _resident(kvg.shape),
                  _resident(wq.shape), _resident(wk.shape), _resident(wv.shape)],
        out_specs=[row(hw), row(hw),
                   pl.BlockSpec((tm // ATTN_TK, wv.shape[0], ATTN_TK), lambda i: (i, 0, 0))],
        out_shape=[jax.ShapeDtypeStruct((t, hw), BF16)] * 2
                  + [jax.ShapeDtypeStruct((t // ATTN_TK, wv.shape[0], ATTN_TK), BF16)],
        compiler_params=_params(),
        name="mla_prep",
    )(z_mla, ct, st, qg, kvg, wq, wk, wv)


def _attn_kernel(q_ref, k_ref, vt_ref, o_ref, *scratch, tq):
    tk = tq // 2
    i = pl.program_id(2)
    per_head = len(scratch) // ATTN_HEADS_PER_STEP
    every, lo, hi = slice(0, tq), slice(0, tk), slice(tk, tq)
    causal = (lax.broadcasted_iota(jnp.int32, (tk, tk), 0)
              <= lax.broadcasted_iota(jnp.int32, (tk, tk), 1))

    def kv_rows(j):
        return pl.ds(pl.multiple_of(j * tk, tk), tk)

    class Head:
        def __init__(self, n):
            (s0, s1, p0, p1, al0, al1, self.m, self.acc) = scratch[n * per_head:(n + 1) * per_head]
            self.s, self.p, self.al = (s0, s1), (p0, p1), (al0, al1)
            self.lanes = slice(n * LANES, (n + 1) * LANES)
            self.vrows = slice(n * ATTN_VROWS, (n + 1) * ATTN_VROWS)

        def scores(self, j, slot, cols=every):
            self.s[slot][:, cols] = lax.dot_general(
                k_ref[kv_rows(j), self.lanes], q_ref[cols, self.lanes],
                (((1,), (1,)), ((), ())), preferred_element_type=F32)

        def accumulate(self, j, slot, cols=every):
            self.acc[:, cols] = self.al[slot][:, cols] * self.acc[:, cols] + jnp.dot(
                vt_ref[j, self.vrows, :], self.p[slot][:, cols], preferred_element_type=F32)

        def softmax(self, slot, cols=every, masked=False):
            s = self.s[slot][:, cols]
            if masked:
                s = jnp.where(causal, s, NEG_BIG)
            m = self.m[:, cols]
            m_new = jnp.maximum(m, jnp.max(s, axis=0, keepdims=True))
            self.p[slot][:, cols] = jnp.exp2(s - m_new).astype(BF16)
            self.al[slot][:, cols] = jnp.exp2(m - m_new)
            self.m[:, cols] = m_new

        def start(self):
            self.m[...] = jnp.full_like(self.m, NEG_BIG)
            self.acc[...] = jnp.zeros_like(self.acc)
            self.p[1][...] = jnp.zeros_like(self.p[1])
            self.al[1][...] = jnp.ones_like(self.al[1])
            self.scores(0, 0)

        def pair(self, p):
            a = 2 * p
            self.scores(a + 1, 1)
            self.accumulate(jnp.maximum(a - 1, 0), 1)
            self.softmax(0)
            self.scores(a + 2, 0)
            self.accumulate(a, 0)
            self.softmax(1)

        def finish(self):
            a = 2 * i
            self.scores(a + 1, 1, hi)
            self.accumulate(jnp.maximum(a - 1, 0), 1)
            self.softmax(0, lo, masked=True)
            self.softmax(0, hi)
            self.accumulate(a, 0)
            self.softmax(1, hi, masked=True)
            self.accumulate(a + 1, 1, hi)
            acc = self.acc[...]
            return (acc[:MLA_V, :] / acc[MLA_V:MLA_V + 1, :]).T

    heads = [Head(n) for n in range(ATTN_HEADS_PER_STEP)]
    for head in heads:
        head.start()

    def pairs(first, count):
        def body(t, carry):
            for head in heads:
                for n in range(count):
                    head.pair(first + count * t + n)
            return carry
        return body

    unrolled = i // ATTN_PAIRS_PER_TRIP
    lax.fori_loop(0, unrolled, pairs(0, ATTN_PAIRS_PER_TRIP), 0)
    lax.fori_loop(0, i - unrolled * ATTN_PAIRS_PER_TRIP, pairs(unrolled * ATTN_PAIRS_PER_TRIP, 1), 0)
    o_ref[...] = jnp.concatenate([head.finish() for head in heads], axis=1).astype(o_ref.dtype)


def _attention(q, k, vt, batch, seq, tq):
    t = q.shape[0]
    nq = seq // tq
    tk = tq // 2
    width = ATTN_HEADS_PER_STEP * LANES
    qspec = pl.BlockSpec((tq, width), lambda b, h, i: (b * nq + i, h))
    kspec = pl.BlockSpec((seq, width), lambda b, h, i: (b, h))
    vtspec = pl.BlockSpec((seq // tk, ATTN_HEADS_PER_STEP * ATTN_VROWS, tk), lambda b, h, i: (b, h, 0))
    per_head = [pltpu.VMEM((tk, tq), F32), pltpu.VMEM((tk, tq), F32),
                pltpu.VMEM((tk, tq), BF16), pltpu.VMEM((tk, tq), BF16),
                pltpu.VMEM((1, tq), F32), pltpu.VMEM((1, tq), F32),
                pltpu.VMEM((1, tq), F32), pltpu.VMEM((ATTN_VROWS, tq), F32)]
    return pl.pallas_call(
        functools.partial(_attn_kernel, tq=tq),
        grid=(batch, MLA_HEADS // ATTN_HEADS_PER_STEP, nq),
        in_specs=[qspec, kspec, vtspec],
        out_specs=pl.BlockSpec((tq, ATTN_HEADS_PER_STEP * MLA_V), lambda b, h, i: (b * nq + i, h)),
        out_shape=jax.ShapeDtypeStruct((t, MLA_HEADS * MLA_V), BF16),
        scratch_shapes=per_head * ATTN_HEADS_PER_STEP,
        compiler_params=_params(3),
        name="mla_attention",
    )(q, k, vt)


def _hgrn_kernel(lbp_ref, hn_ref, qig_ref, f_ref, o_ref, st_ref, b_scr, k_scr, q_scr, v_scr, o_scr,
                 prod_scr, rsum_scr, *, layer, n_chunks):
    C, SUB, K, HALF = HG_CHUNK, HG_SUB, HG_KEY, SUBLANES

    @pl.when(pl.program_id(1) == 0)
    def _():
        st_ref[...] = jnp.zeros_like(st_ref)

    fr = f_ref[...]
    e = jnp.exp(-jnp.abs(fr))
    inv = 1.0 / (1.0 + e)
    log_sig = jnp.minimum(fr, 0.0) + jnp.log(inv)
    sig_neg = jnp.where(fr >= 0.0, e * inv, inv)
    if layer == 0:
        log_f, kk = log_sig, sig_neg
    else:
        lbp = lbp_ref[...]
        w = jnp.exp(lbp - jnp.max(lbp, axis=0, keepdims=True))
        gamma = w / jnp.sum(w, axis=0, keepdims=True)
        lb = gamma[1:2, :]
        for j in range(2, layer + 1):
            lb = lb + gamma[j:j + 1, :]
        log_lb = jnp.log(lb)
        y = jnp.log1p(-lb) + log_sig
        log_f = jnp.maximum(log_lb, y) + jnp.log1p(jnp.exp(-jnp.abs(log_lb - y)))
        kk = (1.0 - lb) * sig_neg
    hq = qig_ref[:, 0:HG_W].astype(F32)
    qq = hq * jax.nn.sigmoid(hq)
    vv = qig_ref[:, HG_W:2 * HG_W].astype(F32)
    log_f = log_f * LOG2E
    p0 = log_f.astype(BF16)
    r1 = log_f - p0.astype(F32)
    p1 = r1.astype(BF16)
    p2 = (r1 - p1.astype(F32)).astype(BF16)
    tri = (lax.broadcasted_iota(jnp.int32, (C, C), 1)
           <= lax.broadcasted_iota(jnp.int32, (C, C), 0)).astype(BF16)
    for c in range(n_chunks):
        rs = slice(c * C, (c + 1) * C)
        b_scr[c] = (jnp.dot(tri, p0[rs], preferred_element_type=F32)
                    + jnp.dot(tri, p1[rs], preferred_element_type=F32)
                    + jnp.dot(tri, p2[rs], preferred_element_type=F32))
        k_scr[c], q_scr[c], v_scr[c] = kk[rs], qq[rs], vv[rs]

    ones = jnp.ones((K, K), BF16)
    t_idx = lax.broadcasted_iota(jnp.int32, (HALF, K), 0)
    keep = [t_idx >= s for s in range(HALF)]

    def chunk(c, carry):
        heads = [slice(h * K, (h + 1) * K) for h in range(HG_HEADS)]
        blocks = range(0, C, SUB)
        bs = [b_scr[c, :, hs] for hs in heads]
        qs = [q_scr[c, :, hs] for hs in heads]
        ks = [k_scr[c, :, hs] for hs in heads]
        vs = [qig_ref[pl.ds(pl.multiple_of(c * C, C), C), HG_W + h * K:HG_W + (h + 1) * K]
              for h in range(HG_HEADS)]

        o_state = []
        for h, hs in enumerate(heads):
            state = st_ref[h]
            o_state.append(lax.dot_general(
                (qs[h] * jnp.exp2(bs[h])).astype(BF16), state.astype(BF16),
                (((1,), (1,)), ((), ())), preferred_element_type=F32))
            blast = bs[h][C - 1:C, :]
            kdec = (ks[h] * jnp.exp2(blast - bs[h])).astype(BF16)
            st_ref[h] = state * jnp.exp2(blast) + jnp.dot(
                v_scr[c, :, hs].T.astype(BF16), kdec, preferred_element_type=F32)

        a_off = {}
        for h in range(HG_HEADS):
            for r in blocks[1:]:
                bref = bs[h][r - 1:r, :]
                qd = (qs[h][r:r + SUB] * jnp.exp2(bs[h][r:r + SUB] - bref)).astype(BF16)
                kd = (ks[h][0:r] * jnp.exp2(bref - bs[h][0:r])).astype(BF16)
                a_off[h, r] = lax.dot_general(qd, kd, (((1,), (1,)), ((), ())),
                                              preferred_element_type=F32)

        owner = []
        for h, hs in enumerate(heads):
            for r in blocks:
                prods = []
                for s in range(SUB):
                    b_row, k_row = b_scr[c, r + s:r + s + 1, hs], k_scr[c, r + s:r + s + 1, hs]
                    for half in range(s // HALF, SUB // HALF):
                        t0 = r + half * HALF
                        arg = bs[h][t0:t0 + HALF] - b_row
                        if half == s // HALF:
                            arg = jnp.where(keep[s % HALF], arg, -jnp.inf)
                        prods.append(qs[h][t0:t0 + HALF] * jnp.exp2(arg) * k_row)
                        owner.append((h, t0, r + s))
                n0 = (len(owner) - len(prods)) * HALF
                prod_scr[n0:n0 + len(prods) * HALF, :] = jnp.concatenate(prods, axis=0).astype(BF16)
        rsum_scr[...] = jnp.dot(prod_scr[...], ones, preferred_element_type=F32)

        o_half = {}
        for h in range(HG_HEADS):
            for r in blocks:
                o_i = o_state[h][r:r + SUB]
                if r > 0:
                    o_i = o_i + jnp.dot(a_off[h, r].astype(BF16), vs[h][0:r],
                                        preferred_element_type=F32)
                for half in range(SUB // HALF):
                    o_half[h, r + half * HALF] = o_i[half * HALF:(half + 1) * HALF]
        for n, (h, t0, s_row) in enumerate(owner):
            o_half[h, t0] = o_half[h, t0] + (rsum_scr[n * HALF:(n + 1) * HALF, :]
                                             * v_scr[c, s_row:s_row + 1, heads[h]])
        for h, hs in enumerate(heads):
            o_scr[c, :, hs] = jnp.concatenate([o_half[h, t0] for t0 in range(0, C, HALF)], axis=0)
        return carry

    lax.fori_loop(0, n_chunks, chunk, 0)

    hn = hn_ref[...]
    for c in range(n_chunks):
        rs = slice(c * C, (c + 1) * C)
        for h in range(HG_HEADS):
            hs = slice(h * K, (h + 1) * K)
            g_h = qig_ref[rs, 2 * HG_W + h * K:2 * HG_W + (h + 1) * K].astype(F32)
            o_ref[rs, hs] = (_rms(o_scr[c, :, hs], hn)
                             * (g_h * jax.nn.sigmoid(g_h))).astype(o_ref.dtype)


def _hgrn(lbp, hn, z_hqig, z_hf, batch, seq, layer, tt):
    t = z_hf.shape[0]
    nt = seq // tt
    row = lambda w: pl.BlockSpec((tt, w), lambda b, i: (b * nt + i, 0))
    tile = pltpu.VMEM((tt // HG_CHUNK, HG_CHUNK, HG_W), F32)
    halves = HG_SUB // SUBLANES
    pair_rows = HG_HEADS * (HG_CHUNK // HG_SUB) * (halves * (halves + 1) // 2 * SUBLANES) * SUBLANES
    return pl.pallas_call(
        functools.partial(_hgrn_kernel, layer=layer, n_chunks=tt // HG_CHUNK),
        grid=(batch, nt),
        in_specs=[_resident(lbp.shape), _resident(hn.shape), row(SEG_HQIG), row(SEG_HF)],
        out_specs=row(HG_W),
        out_shape=jax.ShapeDtypeStruct((t, HG_W), BF16),
        scratch_shapes=[pltpu.VMEM((HG_HEADS, HG_KEY, HG_KEY), F32), tile, tile, tile, tile, tile,
                        pltpu.VMEM((pair_rows, HG_KEY), BF16), pltpu.VMEM((pair_rows, HG_KEY), F32)],
        compiler_params=_params(2),
        name="hgrn2",
    )(lbp, hn, z_hqig, z_hf)


def _gmlp_kernel(z_ref, g_ref, b_ref, ws_ref, bias_ref, o_ref, *, n_chunks):
    z = z_ref[...].astype(F32)
    uv = 0.5 * z * (1.0 + lax.erf(z * (2.0 ** -0.5)))
    u, v = uv[:, :GM_W], uv[:, GM_W:]
    mu = jnp.mean(v, axis=-1, keepdims=True)
    var = jnp.mean(jnp.square(v - mu), axis=-1, keepdims=True)
    vn = ((v - mu) * lax.rsqrt(var + EPS) * g_ref[...] + b_ref[...]).astype(BF16)
    T = GM_CHUNK
    tri = lax.broadcasted_iota(jnp.int32, (T, T), 1) <= lax.broadcasted_iota(jnp.int32, (T, T), 0)
    for g in range(GM_GROUPS):
        gs = slice(g * LANES, (g + 1) * LANES)
        w = jnp.where(tri, ws_ref[g], 0.0).astype(BF16)
        bias = bias_ref[:, gs]
        for n in range(n_chunks):
            rs = slice(n * T, (n + 1) * T)
            mixed = jnp.dot(w, vn[rs, gs], preferred_element_type=F32) + bias
            o_ref[rs, gs] = (u[rs, gs] * mixed).astype(o_ref.dtype)


def _gmlp(z_guv, ln_g, ln_b, w_s, bias_full, tg):
    t = z_guv.shape[0]
    return pl.pallas_call(
        functools.partial(_gmlp_kernel, n_chunks=tg // GM_CHUNK),
        grid=(t // tg,),
        in_specs=[pl.BlockSpec((tg, SEG_GUV), lambda i: (i, 0)), _resident(ln_g.shape),
                  _resident(ln_b.shape), _resident(w_s.shape), _resident(bias_full.shape)],
        out_specs=pl.BlockSpec((tg, GM_W), lambda i: (i, 0)),
        out_shape=jax.ShapeDtypeStruct((t, GM_W), BF16),
        compiler_params=_params(),
        name="gmlp",
    )(z_guv, ln_g, ln_b, w_s, bias_full)


def _memkv_kernel(m_ref, g_ref, w_ref, o_ref):
    h = _rms(m_ref[...], g_ref[...]).astype(BF16)
    o_ref[...] = jnp.dot(h, w_ref[...], preferred_element_type=F32).astype(o_ref.dtype)


def _memkv(mem2, g, w_kv, layer):
    n, d = mem2.shape
    return pl.pallas_call(
        _memkv_kernel,
        grid=(1,),
        in_specs=[_resident(mem2.shape), _resident(g.shape), _layer_resident(w_kv, layer)],
        out_specs=pl.BlockSpec((n, 2 * d), lambda i: (0, 0)),
        out_shape=jax.ShapeDtypeStruct((n, 2 * d), BF16),
        compiler_params=_params(),
        name="mem_kv",
    )(mem2, g, w_kv)


def _merge_xattn_kernel(x_ref, oa_ref, ob_ref, oc_ref, gates_ref, wa_ref, wb_ref, wc_ref, wm_ref,
                        g_ref, wq_ref, kv_ref, wo_ref, o_ref, *, scale):
    d = x_ref.shape[1]
    m = None
    for idx, (br, w) in enumerate(((oa_ref, wa_ref), (ob_ref, wb_ref), (oc_ref, wc_ref))):
        y = jnp.dot(br[...], w[...], preferred_element_type=F32)
        term = jax.nn.sigmoid(gates_ref[:, idx * d:(idx + 1) * d].astype(F32)) * y
        m = term if m is None else m + term
    x = x_ref[...] + jnp.dot(m.astype(BF16), wm_ref[...], preferred_element_type=F32)

    hd = d // X_HEADS
    h = _rms(x, g_ref[...]).astype(BF16)
    q = (jnp.dot(h, wq_ref[...], preferred_element_type=F32) * scale).astype(BF16)
    outs = []
    for hh in range(X_HEADS):
        k = kv_ref[:, hh * hd:(hh + 1) * hd]
        v = kv_ref[:, d + hh * hd:d + (hh + 1) * hd]
        s = lax.dot_general(q[:, hh * hd:(hh + 1) * hd], k, (((1,), (1,)), ((), ())),
                            preferred_element_type=F32)
        p = jnp.exp(s - jnp.max(s, axis=-1, keepdims=True))
        p = p / jnp.sum(p, axis=-1, keepdims=True)
        outs.append(jnp.dot(p.astype(BF16), v, preferred_element_type=F32).astype(BF16))
    o = jnp.concatenate(outs, axis=-1)
    o_ref[...] = x + jnp.dot(o, wo_ref[...], preferred_element_type=F32)


def _merge_xattn(x2, oa, ob, oc, gates, merge_weights, g, wq, kv, wo, layer, seq, n_mem, tm):
    t, d = x2.shape
    per_batch = seq // tm
    row = lambda w: pl.BlockSpec((tm, w), lambda i: (i, 0))
    return pl.pallas_call(
        functools.partial(_merge_xattn_kernel, scale=float((d // X_HEADS) ** -0.5)),
        grid=(t // tm,),
        in_specs=[row(d), row(oa.shape[1]), row(ob.shape[1]), row(oc.shape[1]), row(SEG_GATES)]
                 + [_layer_resident(w, layer) for w in merge_weights]
                 + [_resident(g.shape), _layer_resident(wq, layer),
                    pl.BlockSpec((n_mem, 2 * d), lambda i: (i // per_batch, 0)),
                    _layer_resident(wo, layer)],
        out_specs=row(d),
        out_shape=jax.ShapeDtypeStruct((t, d), F32),
        compiler_params=_params(),
        name="merge_cross_attn",
    )(x2, oa, ob, oc, gates, *merge_weights, g, wq, kv, wo)


def _ffn_kernel(x_ref, g_ref, wi_ref, wo_ref, fg_ref, o_ref, *, d_ff, n_split, final):
    x = x_ref[...]
    h = _rms(x, g_ref[...]).astype(BF16)
    acc = x
    tiles = d_ff // MXU_TILE
    edges = [(c * tiles // n_split) * MXU_TILE for c in range(n_split + 1)]
    for c0, c1 in zip(edges[:-1], edges[1:]):
        gt = jnp.dot(h, wi_ref[:, c0:c1], preferred_element_type=F32)
        up = jnp.dot(h, wi_ref[:, d_ff + c0:d_ff + c1], preferred_element_type=F32)
        a = (gt * jax.nn.sigmoid(gt) * up).astype(BF16)
        acc = acc + jnp.dot(a, wo_ref[c0:c1, :], preferred_element_type=F32)
    o_ref[...] = _rms(acc, fg_ref[...]) if final else acc


def _ffn(x2, g, w_in, w_out, final_g, layer, tm):
    t, d = x2.shape
    d_ff = w_out.shape[1]
    final = layer == w_in.shape[0] - 1
    row = pl.BlockSpec((tm, d), lambda i: (i, 0))
    return pl.pallas_call(
        functools.partial(_ffn_kernel, d_ff=d_ff, n_split=2, final=final),
        grid=(t // tm,),
        in_specs=[row, _resident(g.shape), _layer_resident(w_in, layer), _layer_resident(w_out, layer),
                  _resident(final_g.shape)],
        out_specs=row,
        out_shape=jax.ShapeDtypeStruct((t, d), F32),
        compiler_params=_params(),
        name="ffn",
    )(x2, g, w_in, w_out, final_g)


def _pad_heads(w, heads, width):
    r = w.shape[0]
    w = w.reshape(r, heads, width)
    return jnp.pad(w, ((0, 0), (0, 0), (0, LANES - width))).reshape(r, heads * LANES)


def _rope_slab(w_rope):
    half = MLA_ROPE // 2
    pad = ((0, 0), (0, 0), (MLA_NOPE, LANES - MLA_NOPE - MLA_ROPE))
    swapped = jnp.concatenate([w_rope[..., half:], w_rope[..., :half]], axis=-1)
    return jnp.pad(w_rope, pad), jnp.pad(swapped, pad)


def _layer_weights(l, w_in, mla_w_uq, mla_w_ukv):
    w = w_in[l]
    d = w.shape[0]
    o = [0]
    for s in (MLA_Q_RANK + MLA_KV_RANK, MLA_ROPE, 4 * HG_W, 2 * GM_W, 3 * d):
        o.append(o[-1] + s)
    latents, k_r, hgrn, guv, gates = (w[:, o[i]:o[i + 1]].astype(BF16) for i in range(5))
    kr, kr_rot = _rope_slab(k_r.reshape(d, 1, MLA_ROPE))
    w_mla = jnp.concatenate([latents, kr.reshape(d, LANES), kr_rot.reshape(d, LANES)], axis=1)

    uq = mla_w_uq[l].astype(BF16).reshape(MLA_Q_RANK, MLA_HEADS, MLA_NOPE + MLA_ROPE)
    half = MLA_ROPE // 2
    rope = uq[..., MLA_NOPE:]
    wq = jnp.concatenate([uq, rope[..., half:], rope[..., :half]], axis=-1)
    hw = MLA_HEADS * LANES
    wq = wq.reshape(MLA_Q_RANK, hw)
    ukv = mla_w_ukv[l].astype(BF16).reshape(MLA_KV_RANK, MLA_HEADS, MLA_NOPE + MLA_V)
    wk = _pad_heads(ukv[..., :MLA_NOPE].reshape(MLA_KV_RANK, -1), MLA_HEADS, MLA_NOPE)
    wv = jnp.pad(ukv[..., MLA_NOPE:], ((0, 0), (0, 0), (0, ATTN_VROWS - MLA_V)))
    wv = wv.reshape(MLA_KV_RANK, MLA_HEADS * ATTN_VROWS).T
    return (w_mla, hgrn, guv, gates), (wq, wk, wv)


def kernel(x, mem, positions, mix_norm, w_in, mla_q_norm, mla_w_uq, mla_kv_norm, mla_w_ukv,
           hg_lower_bounds, hg_head_norm, gm_ln_g, gm_ln_b, gm_w_s, gm_b_s,
           w_branch_a, w_branch_b, w_branch_c, w_mix_out,
           xa_norm, mem_norm, xa_w_q, xa_w_kv, xa_w_o,
           ffn_norm, ffn_w_in, ffn_w_out, final_norm):
    batch, seq, d = x.shape
    depth = w_in.shape[0]
    n_mem = mem.shape[1]
    t = batch * seq
    x2 = x.reshape(t, d)
    mem2 = mem.reshape(batch * n_mem, d)
    vec = lambda a: a.reshape(1, -1).astype(F32)

    tm = min(512, seq)
    inv_freq = ROPE_BASE ** (-jnp.arange(0, MLA_ROPE, 2, dtype=F32) / MLA_ROPE)
    invf_lane = jnp.zeros((LANES,), F32).at[MLA_NOPE:MLA_NOPE + MLA_ROPE].set(
        jnp.concatenate([inv_freq, inv_freq])).reshape(1, LANES)
    ct, st = _rope_tables(positions.astype(F32).reshape(t, 1), invf_lane, tm)
    merge_b = [w.astype(BF16) for w in (w_branch_a, w_branch_b, w_branch_c, w_mix_out)]
    xa_q_b, xa_kv_b, xa_o_b = (w.astype(BF16) for w in (xa_w_q, xa_w_kv, xa_w_o))
    ffn_in_b, ffn_out_b = ffn_w_in.astype(BF16), ffn_w_out.astype(BF16)

    for l in range(depth):
        in_weights, mla_weights = _layer_weights(l, w_in, mla_w_uq, mla_w_ukv)
        z_mla, z_hqig, z_hf, z_guv, z_gates = _inproj(x2, vec(mix_norm[l]), in_weights, tm)
        q, k, v = _mla_prep(z_mla, ct, st, vec(mla_q_norm[l]), vec(mla_kv_norm[l]), *mla_weights, tm)
        oa = _attention(q, k, v, batch, seq, min(ATTN_TQ, seq))
        ob = _hgrn(hg_lower_bounds.astype(F32), vec(hg_head_norm[l]), z_hqig, z_hf, batch, seq, l, tm)
        bias_full = jnp.repeat(gm_b_s[l].T, GM_W // GM_GROUPS, axis=1)
        oc = _gmlp(z_guv, vec(gm_ln_g[l]), vec(gm_ln_b[l]), gm_w_s[l], bias_full, tm)
        kv = _memkv(mem2, vec(mem_norm[l]), xa_kv_b, l)
        x2 = _merge_xattn(x2, oa, ob, oc, z_gates, merge_b, vec(xa_norm[l]), xa_q_b, kv, xa_o_b,
                          l, seq, n_mem, tm)
        x2 = _ffn(x2, vec(ffn_norm[l]), ffn_in_b, ffn_out_b, vec(final_norm), l, tm)
    return x2.reshape(batch, seq, d)
```

```python
import functools
import math

import jax
import jax.numpy as jnp
from jax import lax
from jax.experimental import pallas as pl
from jax.experimental.pallas import tpu as pltpu

F32 = jnp.float32
BF16 = jnp.bfloat16

EPS = 1e-6
LANES = 128
SUBLANES = 8
MXU_TILE = 256
LOG2E = math.log2(math.e)
VMEM_LIMIT = 56 * 1024 * 1024

MLA_HEADS = 8
MLA_Q_RANK = 384
MLA_KV_RANK = 256
MLA_NOPE = 64
MLA_ROPE = 32
MLA_V = 64
ROPE_BASE = 10000.0
HG_HEADS = 4
HG_KEY = 128
HG_CHUNK = 64
HG_SUB = 16
HG_W = HG_HEADS * HG_KEY
GM_GROUPS = 4
GM_CHUNK = 128
GM_W = 512
X_HEADS = 4
NEG_BIG = -1e30
ATTN_HEADS_PER_STEP = 4
ATTN_PAIRS_PER_TRIP = 2
ATTN_TQ = 512
ATTN_TK = ATTN_TQ // 2

SEG_MLA = MLA_Q_RANK + MLA_KV_RANK + 2 * LANES
SEG_HQIG = 3 * HG_W
SEG_HF = HG_W
SEG_GATES = 3 * 1024


def _params(n_axes=1):
    return pltpu.CompilerParams(
        dimension_semantics=("arbitrary",) * n_axes, vmem_limit_bytes=VMEM_LIMIT)


def _rms(x, g):
    return x * lax.rsqrt(jnp.mean(x * x, axis=-1, keepdims=True) + EPS) * g


def _resident(shape):
    zeros = (0,) * len(shape)
    return pl.BlockSpec(shape, lambda *_: zeros, pipeline_mode=pl.Buffered(1))


def _layer_resident(stacked, layer):
    index = (layer, 0, 0)
    return pl.BlockSpec((None,) + stacked.shape[1:], lambda *_: index, pipeline_mode=pl.Buffered(1))


def _rope_kernel(pos_ref, invf_ref, ct_ref, st_ref):
    ang = pos_ref[...] * invf_ref[...]
    lane = lax.broadcasted_iota(jnp.int32, ang.shape, 1)
    c, s = jnp.cos(ang), jnp.sin(ang)
    in_rope = (lane >= MLA_NOPE) & (lane < MLA_NOPE + MLA_ROPE)
    first = lane < MLA_NOPE + MLA_ROPE // 2
    ct_ref[...] = jnp.where(lane < MLA_NOPE, 1.0, jnp.where(in_rope, c, 0.0))
    st_ref[...] = jnp.where(in_rope, jnp.where(first, -s, s), 0.0)


def _rope_tables(pos_f, invf_lane, tm):
    t = pos_f.shape[0]
    return pl.pallas_call(
        _rope_kernel,
        grid=(t // tm,),
        in_specs=[pl.BlockSpec((tm, 1), lambda i: (i, 0)), _resident((1, LANES))],
        out_specs=[pl.BlockSpec((tm, LANES), lambda i: (i, 0))] * 2,
        out_shape=[jax.ShapeDtypeStruct((t, LANES), F32)] * 2,
        compiler_params=_params(),
        name="rope_tables",
    )(pos_f, invf_lane)


def _inproj_kernel(x_ref, g_ref, wm_ref, wh_ref, wguv_ref, wgates_ref, ln_g_ref, ln_b_ref, ws_ref,
                   bias_ref, mla_ref, hqig_ref, hf_ref, oc_ref, gates_ref, *, nchunk):
    h = _rms(x_ref[...], g_ref[...]).astype(BF16)

    def project(w_ref, c0, c1):
        return jnp.dot(h, w_ref[:, c0:c1], preferred_element_type=F32)

    def gelu(z):
        return 0.5 * z * (1.0 + lax.erf(z * (2.0 ** -0.5)))

    for c, (out, o0) in enumerate(((hqig_ref, 0), (hf_ref, 0), (hqig_ref, HG_W), (hqig_ref, 2 * HG_W))):
        out[:, o0:o0 + HG_W] = project(wh_ref, c * HG_W, (c + 1) * HG_W).astype(out.dtype)

    u, v = gelu(project(wguv_ref, 0, GM_W)), gelu(project(wguv_ref, GM_W, 2 * GM_W))
    mu = jnp.mean(v, axis=-1, keepdims=True)
    var = jnp.mean(jnp.square(v - mu), axis=-1, keepdims=True)
    vn = ((v - mu) * lax.rsqrt(var + EPS) * ln_g_ref[...] + ln_b_ref[...]).astype(BF16)
    T = GM_CHUNK
    tri = lax.broadcasted_iota(jnp.int32, (T, T), 1) <= lax.broadcasted_iota(jnp.int32, (T, T), 0)
    for g in range(GM_GROUPS):
        gs = slice(g * LANES, (g + 1) * LANES)
        w = jnp.where(tri, ws_ref[g], 0.0).astype(BF16)
        bias = bias_ref[:, gs]
        for n in range(x_ref.shape[0] // T):
            rs = slice(n * T, (n + 1) * T)
            mixed = jnp.dot(w, vn[rs, gs], preferred_element_type=F32) + bias
            oc_ref[rs, gs] = (u[rs, gs] * mixed).astype(oc_ref.dtype)

    for out, w_ref in ((mla_ref, wm_ref), (gates_ref, wgates_ref)):
        width = out.shape[1]
        for c0 in range(0, width, nchunk):
            c1 = min(c0 + nchunk, width)
            out[:, c0:c1] = project(w_ref, c0, c1).astype(out.dtype)


def _inproj(x2, g, weights, gmlp_params, tm):
    t, d = x2.shape
    widths = (SEG_MLA, SEG_HQIG, SEG_HF, GM_W, SEG_GATES)
    dtypes = (BF16, BF16, F32, BF16, BF16)
    consts = tuple(weights) + tuple(gmlp_params)
    return pl.pallas_call(
        functools.partial(_inproj_kernel, nchunk=512),
        grid=(t // tm,),
        in_specs=[pl.BlockSpec((tm, d), lambda i: (i, 0)), _resident((1, d))]
                 + [_resident(w.shape) for w in consts],
        out_specs=[pl.BlockSpec((tm, w), lambda i: (i, 0)) for w in widths],
        out_shape=[jax.ShapeDtypeStruct((t, w), dt) for w, dt in zip(widths, dtypes)],
        compiler_params=_params(),
        name="in_proj",
    )(x2, g, *consts)


def _mla_prep_kernel(z_ref, ct_ref, st_ref, qg_ref, kvg_ref, wq_ref, wk_ref, wv_ref,
                     q_ref, k_ref, v_ref, *, scale):
    ct, st = ct_ref[...], st_ref[...]
    cq = z_ref[:, :MLA_Q_RANK].astype(F32)
    qn = _rms(cq, qg_ref[...]).astype(BF16)
    q = jnp.dot(qn, wq_ref[...], preferred_element_type=F32)
    ckv = z_ref[:, MLA_Q_RANK:MLA_Q_RANK + MLA_KV_RANK].astype(F32)
    kvn = _rms(ckv, kvg_ref[...]).astype(BF16)
    kn = jnp.dot(kvn, wk_ref[...], preferred_element_type=F32)
    vt = lax.dot_general(wv_ref[...], kvn, (((1,), (1,)), ((), ())), preferred_element_type=F32)
    row = lax.broadcasted_iota(jnp.int32, vt.shape, 0)
    vt = jnp.where(row % LANES == MLA_V, 1.0, vt).astype(BF16)
    for n in range(v_ref.shape[0]):
        v_ref[n] = vt[:, n * ATTN_TK:(n + 1) * ATTN_TK]
    o = MLA_Q_RANK + MLA_KV_RANK
    krope = z_ref[:, o:o + LANES].astype(F32) * ct + z_ref[:, o + LANES:o + 2 * LANES].astype(F32) * st
    for h in range(MLA_HEADS):
        sl = slice(h * LANES, (h + 1) * LANES)
        q_h = q[:, sl]
        swapped = pltpu.roll(q_h, LANES - MLA_ROPE, axis=1)
        q_ref[:, sl] = ((q_h * ct + swapped * st) * scale).astype(BF16)
        k_ref[:, sl] = (kn[:, sl] + krope).astype(BF16)


def _mla_prep(z_mla, ct, st, qg, kvg, wq, wk, wv, tm):
    t = z_mla.shape[0]
    hw = MLA_HEADS * LANES
    row = lambda w: pl.BlockSpec((tm, w), lambda i: (i, 0))
    return pl.pallas_call(
        functools.partial(_mla_prep_kernel, scale=float((MLA_NOPE + MLA_ROPE) ** -0.5 * math.log2(math.e))),
        grid=(t // tm,),
        in_specs=[row(SEG_MLA), row(LANES), row(LANES), _resident(qg.shape), _resident(kvg.shape),
                  _resident(wq.shape), _resident(wk.shape), _resident(wv.shape)],
        out_specs=[row(hw), row(hw), pl.BlockSpec((tm // ATTN_TK, hw, ATTN_TK), lambda i: (i, 0, 0))],
        out_shape=[jax.ShapeDtypeStruct((t, hw), BF16)] * 2
                  + [jax.ShapeDtypeStruct((t // ATTN_TK, hw, ATTN_TK), BF16)],
        compiler_params=_params(),
        name="mla_prep",
    )(z_mla, ct, st, qg, kvg, wq, wk, wv)


def _attn_kernel(q_ref, k_ref, vt_ref, o_ref, *scratch, tq):
    tk = tq // 2
    i = pl.program_id(2)
    per_head = len(scratch) // ATTN_HEADS_PER_STEP
    every, lo, hi = slice(0, tq), slice(0, tk), slice(tk, tq)
    causal = (lax.broadcasted_iota(jnp.int32, (tk, tk), 0)
              <= lax.broadcasted_iota(jnp.int32, (tk, tk), 1))

    def kv_rows(j):
        return pl.ds(pl.multiple_of(j * tk, tk), tk)

    class Head:
        def __init__(self, n):
            (s0, s1, p0, p1, al0, al1, self.m, self.acc) = scratch[n * per_head:(n + 1) * per_head]
            self.s, self.p, self.al = (s0, s1), (p0, p1), (al0, al1)
            self.lanes = slice(n * LANES, (n + 1) * LANES)

        def scores(self, j, slot, cols=every):
            self.s[slot][:, cols] = lax.dot_general(
                k_ref[kv_rows(j), self.lanes], q_ref[cols, self.lanes],
                (((1,), (1,)), ((), ())), preferred_element_type=F32)

        def accumulate(self, j, slot, cols=every):
            self.acc[:, cols] = self.al[slot][:, cols] * self.acc[:, cols] + jnp.dot(
                vt_ref[j, self.lanes, :], self.p[slot][:, cols], preferred_element_type=F32)

        def softmax(self, slot, cols=every, masked=False):
            s = self.s[slot][:, cols]
            if masked:
                s = jnp.where(causal, s, NEG_BIG)
            m = self.m[:, cols]
            m_new = jnp.maximum(m, jnp.max(s, axis=0, keepdims=True))
            self.p[slot][:, cols] = jnp.exp2(s - m_new).astype(BF16)
            self.al[slot][:, cols] = jnp.exp2(m - m_new)
            self.m[:, cols] = m_new

        def start(self):
            self.m[...] = jnp.full_like(self.m, NEG_BIG)
            self.acc[...] = jnp.zeros_like(self.acc)
            self.p[1][...] = jnp.zeros_like(self.p[1])
            self.al[1][...] = jnp.ones_like(self.al[1])
            self.scores(0, 0)

        def pair(self, p):
            a = 2 * p
            self.scores(a + 1, 1)
            self.accumulate(jnp.maximum(a - 1, 0), 1)
            self.softmax(0)
            self.scores(a + 2, 0)
            self.accumulate(a, 0)
            self.softmax(1)

        def finish(self):
            a = 2 * i
            self.scores(a + 1, 1, hi)
            self.accumulate(jnp.maximum(a - 1, 0), 1)
            self.softmax(0, lo, masked=True)
            self.softmax(0, hi)
            self.accumulate(a, 0)
            self.softmax(1, hi, masked=True)
            self.accumulate(a + 1, 1, hi)
            acc = self.acc[...]
            return (acc / acc[MLA_V:MLA_V + 1, :]).T[:, :MLA_V]

    heads = [Head(n) for n in range(ATTN_HEADS_PER_STEP)]
    for head in heads:
        head.start()

    def pairs(first, count):
        def body(t, carry):
            for head in heads:
                for n in range(count):
                    head.pair(first + count * t + n)
            return carry
        return body

    unrolled = i // ATTN_PAIRS_PER_TRIP
    lax.fori_loop(0, unrolled, pairs(0, ATTN_PAIRS_PER_TRIP), 0)
    lax.fori_loop(0, i - unrolled * ATTN_PAIRS_PER_TRIP, pairs(unrolled * ATTN_PAIRS_PER_TRIP, 1), 0)
    o_ref[...] = jnp.concatenate([head.finish() for head in heads], axis=1).astype(o_ref.dtype)


def _attention(q, k, vt, batch, seq, tq):
    t = q.shape[0]
    nq = seq // tq
    tk = tq // 2
    width = ATTN_HEADS_PER_STEP * LANES
    qspec = pl.BlockSpec((tq, width), lambda b, h, i: (b * nq + i, h))
    kspec = pl.BlockSpec((seq, width), lambda b, h, i: (b, h))
    vtspec = pl.BlockSpec((seq // tk, width, tk), lambda b, h, i: (b, h, 0))
    per_head = [pltpu.VMEM((tk, tq), F32), pltpu.VMEM((tk, tq), F32),
                pltpu.VMEM((tk, tq), BF16), pltpu.VMEM((tk, tq), BF16),
                pltpu.VMEM((1, tq), F32), pltpu.VMEM((1, tq), F32),
                pltpu.VMEM((1, tq), F32), pltpu.VMEM((LANES, tq), F32)]
    return pl.pallas_call(
        functools.partial(_attn_kernel, tq=tq),
        grid=(batch, MLA_HEADS // ATTN_HEADS_PER_STEP, nq),
        in_specs=[qspec, kspec, vtspec],
        out_specs=pl.BlockSpec((tq, ATTN_HEADS_PER_STEP * MLA_V), lambda b, h, i: (b * nq + i, h)),
        out_shape=jax.ShapeDtypeStruct((t, MLA_HEADS * MLA_V), BF16),
        scratch_shapes=per_head * ATTN_HEADS_PER_STEP,
        compiler_params=_params(3),
        name="mla_attention",
    )(q, k, vt)


def _hgrn_kernel(lbp_ref, hn_ref, qig_ref, f_ref, o_ref, st_ref, b_scr, k_scr, q_scr, v_scr, o_scr,
                 prod_scr, rsum_scr, *, layer, n_chunks):
    C, SUB, K, HALF = HG_CHUNK, HG_SUB, HG_KEY, SUBLANES

    @pl.when(pl.program_id(1) == 0)
    def _():
        st_ref[...] = jnp.zeros_like(st_ref)

    fr = f_ref[...]
    e = jnp.exp(-jnp.abs(fr))
    inv = 1.0 / (1.0 + e)
    log_sig = jnp.minimum(fr, 0.0) + jnp.log(inv)
    sig_neg = jnp.where(fr >= 0.0, e * inv, inv)
    if layer == 0:
        log_f, kk = log_sig, sig_neg
    else:
        lbp = lbp_ref[...]
        w = jnp.exp(lbp - jnp.max(lbp, axis=0, keepdims=True))
        gamma = w / jnp.sum(w, axis=0, keepdims=True)
        lb = gamma[1:2, :]
        for j in range(2, layer + 1):
            lb = lb + gamma[j:j + 1, :]
        log_lb = jnp.log(lb)
        y = jnp.log1p(-lb) + log_sig
        log_f = jnp.maximum(log_lb, y) + jnp.log1p(jnp.exp(-jnp.abs(log_lb - y)))
        kk = (1.0 - lb) * sig_neg
    hq = qig_ref[:, 0:HG_W].astype(F32)
    qq = hq * jax.nn.sigmoid(hq)
    vv = qig_ref[:, HG_W:2 * HG_W].astype(F32)
    log_f = log_f * LOG2E
    p0 = log_f.astype(BF16)
    r1 = log_f - p0.astype(F32)
    p1 = r1.astype(BF16)
    p2 = (r1 - p1.astype(F32)).astype(BF16)
    tri = (lax.broadcasted_iota(jnp.int32, (C, C), 1)
           <= lax.broadcasted_iota(jnp.int32, (C, C), 0)).astype(BF16)
    for c in range(n_chunks):
        rs = slice(c * C, (c + 1) * C)
        b_scr[c] = (jnp.dot(tri, p0[rs], preferred_element_type=F32)
                    + jnp.dot(tri, p1[rs], preferred_element_type=F32)
                    + jnp.dot(tri, p2[rs], preferred_element_type=F32))
        k_scr[c], q_scr[c], v_scr[c] = kk[rs], qq[rs], vv[rs]

    ones = jnp.ones((K, K), BF16)
    t_idx = lax.broadcasted_iota(jnp.int32, (HALF, K), 0)
    keep = [t_idx >= s for s in range(HALF)]

    def chunk(c, carry):
        heads = [slice(h * K, (h + 1) * K) for h in range(HG_HEADS)]
        blocks = range(0, C, SUB)
        bs = [b_scr[c, :, hs] for hs in heads]
        qs = [q_scr[c, :, hs] for hs in heads]
        ks = [k_scr[c, :, hs] for hs in heads]
        vs = [qig_ref[pl.ds(pl.multiple_of(c * C, C), C), HG_W + h * K:HG_W + (h + 1) * K]
              for h in range(HG_HEADS)]

        o_state = []
        for h, hs in enumerate(heads):
            state = st_ref[h]
            o_state.append(lax.dot_general(
                (qs[h] * jnp.exp2(bs[h])).astype(BF16), state.astype(BF16),
                (((1,), (1,)), ((), ())), preferred_element_type=F32))
            blast = bs[h][C - 1:C, :]
            kdec = (ks[h] * jnp.exp2(blast - bs[h])).astype(BF16)
            st_ref[h] = state * jnp.exp2(blast) + jnp.dot(
                v_scr[c, :, hs].T.astype(BF16), kdec, preferred_element_type=F32)

        a_off = {}
        for h in range(HG_HEADS):
            for r in blocks[1:]:
                bref = bs[h][r - 1:r, :]
                qd = (qs[h][r:r + SUB] * jnp.exp2(bs[h][r:r + SUB] - bref)).astype(BF16)
                kd = (ks[h][0:r] * jnp.exp2(bref - bs[h][0:r])).astype(BF16)
                a_off[h, r] = lax.dot_general(qd, kd, (((1,), (1,)), ((), ())),
                                              preferred_element_type=F32)

        owner = []
        for h, hs in enumerate(heads):
            for r in blocks:
                prods = []
                for s in range(SUB):
                    b_row, k_row = b_scr[c, r + s:r + s + 1, hs], k_scr[c, r + s:r + s + 1, hs]
                    for half in range(s // HALF, SUB // HALF):
                        t0 = r + half * HALF
                        arg = bs[h][t0:t0 + HALF] - b_row
                        if half == s // HALF:
                            arg = jnp.where(keep[s % HALF], arg, -jnp.inf)
                        prods.append(qs[h][t0:t0 + HALF] * jnp.exp2(arg) * k_row)
                        owner.append((h, t0, r + s))
                n0 = (len(owner) - len(prods)) * HALF
                prod_scr[n0:n0 + len(prods) * HALF, :] = jnp.concatenate(prods, axis=0).astype(BF16)
        rsum_scr[...] = jnp.dot(prod_scr[...], ones, preferred_element_type=F32)

        o_half = {}
        for h in range(HG_HEADS):
            for r in blocks:
                o_i = o_state[h][r:r + SUB]
                if r > 0:
                    o_i = o_i + jnp.dot(a_off[h, r].astype(BF16), vs[h][0:r],
                                        preferred_element_type=F32)
                for half in range(SUB // HALF):
                    o_half[h, r + half * HALF] = o_i[half * HALF:(half + 1) * HALF]
        for n, (h, t0, s_row) in enumerate(owner):
            o_half[h, t0] = o_half[h, t0] + (rsum_scr[n * HALF:(n + 1) * HALF, :]
                                             * v_scr[c, s_row:s_row + 1, heads[h]])
        for h, hs in enumerate(heads):
            o_scr[c, :, hs] = jnp.concatenate([o_half[h, t0] for t0 in range(0, C, HALF)], axis=0)
        return carry

    lax.fori_loop(0, n_chunks, chunk, 0)

    hn = hn_ref[...]
    for c in range(n_chunks):
        rs = slice(c * C, (c + 1) * C)
        for h in range(HG_HEADS):
            hs = slice(h * K, (h + 1) * K)
            g_h = qig_ref[rs, 2 * HG_W + h * K:2 * HG_W + (h + 1) * K].astype(F32)
            o_ref[rs, hs] = (_rms(o_scr[c, :, hs], hn)
                             * (g_h * jax.nn.sigmoid(g_h))).astype(o_ref.dtype)


def _hgrn(lbp, hn, z_hqig, z_hf, batch, seq, layer, tt):
    t = z_hf.shape[0]
    nt = seq // tt
    row = lambda w: pl.BlockSpec((tt, w), lambda b, i: (b * nt + i, 0))
    tile = pltpu.VMEM((tt // HG_CHUNK, HG_CHUNK, HG_W), F32)
    halves = HG_SUB // SUBLANES
    pair_rows = HG_HEADS * (HG_CHUNK // HG_SUB) * (halves * (halves + 1) // 2 * SUBLANES) * SUBLANES
    return pl.pallas_call(
        functools.partial(_hgrn_kernel, layer=layer, n_chunks=tt // HG_CHUNK),
        grid=(batch, nt),
        in_specs=[_resident(lbp.shape), _resident(hn.shape), row(SEG_HQIG), row(SEG_HF)],
        out_specs=row(HG_W),
        out_shape=jax.ShapeDtypeStruct((t, HG_W), BF16),
        scratch_shapes=[pltpu.VMEM((HG_HEADS, HG_KEY, HG_KEY), F32), tile, tile, tile, tile, tile,
                        pltpu.VMEM((pair_rows, HG_KEY), BF16), pltpu.VMEM((pair_rows, HG_KEY), F32)],
        compiler_params=_params(2),
        name="hgrn2",
    )(lbp, hn, z_hqig, z_hf)


def _memkv_kernel(m_ref, g_ref, w_ref, o_ref):
    h = _rms(m_ref[...], g_ref[...]).astype(BF16)
    o_ref[...] = jnp.dot(h, w_ref[...], preferred_element_type=F32).astype(o_ref.dtype)


def _memkv(mem2, g, w_kv, layer):
    n, d = mem2.shape
    return pl.pallas_call(
        _memkv_kernel,
        grid=(1,),
        in_specs=[_resident(mem2.shape), _resident(g.shape), _layer_resident(w_kv, layer)],
        out_specs=pl.BlockSpec((n, 2 * d), lambda i: (0, 0)),
        out_shape=jax.ShapeDtypeStruct((n, 2 * d), BF16),
        compiler_params=_params(),
        name="mem_kv",
    )(mem2, g, w_kv)


def _merge_xattn_kernel(x_ref, oa_ref, ob_ref, oc_ref, gates_ref, wa_ref, wb_ref, wc_ref, wm_ref,
                        g_ref, wq_ref, kv_ref, wo_ref, o_ref, *, scale):
    d = x_ref.shape[1]
    m = None
    for idx, (br, w) in enumerate(((oa_ref, wa_ref), (ob_ref, wb_ref), (oc_ref, wc_ref))):
        y = jnp.dot(br[...], w[...], preferred_element_type=F32)
        term = jax.nn.sigmoid(gates_ref[:, idx * d:(idx + 1) * d].astype(F32)) * y
        m = term if m is None else m + term
    x = x_ref[...] + jnp.dot(m.astype(BF16), wm_ref[...], preferred_element_type=F32)

    hd = d // X_HEADS
    h = _rms(x, g_ref[...]).astype(BF16)
    q = (jnp.dot(h, wq_ref[...], preferred_element_type=F32) * scale).astype(BF16)
    outs = []
    for hh in range(X_HEADS):
        k = kv_ref[:, hh * hd:(hh + 1) * hd]
        v = kv_ref[:, d + hh * hd:d + (hh + 1) * hd]
        s = lax.dot_general(q[:, hh * hd:(hh + 1) * hd], k, (((1,), (1,)), ((), ())),
                            preferred_element_type=F32)
        p = jnp.exp(s - jnp.max(s, axis=-1, keepdims=True))
        p = p / jnp.sum(p, axis=-1, keepdims=True)
        outs.append(jnp.dot(p.astype(BF16), v, preferred_element_type=F32).astype(BF16))
    o = jnp.concatenate(outs, axis=-1)
    o_ref[...] = x + jnp.dot(o, wo_ref[...], preferred_element_type=F32)


def _merge_xattn(x2, oa, ob, oc, gates, merge_weights, g, wq, kv, wo, layer, seq, n_mem, tm):
    t, d = x2.shape
    per_batch = seq // tm
    row = lambda w: pl.BlockSpec((tm, w), lambda i: (i, 0))
    return pl.pallas_call(
        functools.partial(_merge_xattn_kernel, scale=float((d // X_HEADS) ** -0.5)),
        grid=(t // tm,),
        in_specs=[row(d), row(oa.shape[1]), row(ob.shape[1]), row(oc.shape[1]), row(SEG_GATES)]
                 + [_layer_resident(w, layer) for w in merge_weights]
                 + [_resident(g.shape), _layer_resident(wq, layer),
                    pl.BlockSpec((n_mem, 2 * d), lambda i: (i // per_batch, 0)),
                    _layer_resident(wo, layer)],
        out_specs=row(d),
        out_shape=jax.ShapeDtypeStruct((t, d), F32),
        compiler_params=_params(),
        name="merge_cross_attn",
    )(x2, oa, ob, oc, gates, *merge_weights, g, wq, kv, wo)


def _ffn_kernel(x_ref, g_ref, wi_ref, wo_ref, fg_ref, o_ref, *, d_ff, n_split, final):
    x = x_ref[...]
    h = _rms(x, g_ref[...]).astype(BF16)
    acc = x
    tiles = d_ff // MXU_TILE
    edges = [(c * tiles // n_split) * MXU_TILE for c in range(n_split + 1)]
    for c0, c1 in zip(edges[:-1], edges[1:]):
        gt = jnp.dot(h, wi_ref[:, c0:c1], preferred_element_type=F32)
        up = jnp.dot(h, wi_ref[:, d_ff + c0:d_ff + c1], preferred_element_type=F32)
        a = (gt * jax.nn.sigmoid(gt) * up).astype(BF16)
        acc = acc + jnp.dot(a, wo_ref[c0:c1, :], preferred_element_type=F32)
    o_ref[...] = _rms(acc, fg_ref[...]) if final else acc


def _ffn(x2, g, w_in, w_out, final_g, layer, tm):
    t, d = x2.shape
    d_ff = w_out.shape[1]
    final = layer == w_in.shape[0] - 1
    row = pl.BlockSpec((tm, d), lambda i: (i, 0))
    return pl.pallas_call(
        functools.partial(_ffn_kernel, d_ff=d_ff, n_split=2, final=final),
        grid=(t // tm,),
        in_specs=[row, _resident(g.shape), _layer_resident(w_in, layer), _layer_resident(w_out, layer),
                  _resident(final_g.shape)],
        out_specs=row,
        out_shape=jax.ShapeDtypeStruct((t, d), F32),
        compiler_params=_params(),
        name="ffn",
    )(x2, g, w_in, w_out, final_g)


def _pad_heads(w, heads, width):
    r = w.shape[0]
    w = w.reshape(r, heads, width)
    return jnp.pad(w, ((0, 0), (0, 0), (0, LANES - width))).reshape(r, heads * LANES)


def _rope_slab(w_rope):
    half = MLA_ROPE // 2
    pad = ((0, 0), (0, 0), (MLA_NOPE, LANES - MLA_NOPE - MLA_ROPE))
    swapped = jnp.concatenate([w_rope[..., half:], w_rope[..., :half]], axis=-1)
    return jnp.pad(w_rope, pad), jnp.pad(swapped, pad)


def _layer_weights(l, w_in, mla_w_uq, mla_w_ukv):
    w = w_in[l]
    d = w.shape[0]
    o = [0]
    for s in (MLA_Q_RANK + MLA_KV_RANK, MLA_ROPE, 4 * HG_W, 2 * GM_W, 3 * d):
        o.append(o[-1] + s)
    latents, k_r, hgrn, guv, gates = (w[:, o[i]:o[i + 1]].astype(BF16) for i in range(5))
    kr, kr_rot = _rope_slab(k_r.reshape(d, 1, MLA_ROPE))
    w_mla = jnp.concatenate([latents, kr.reshape(d, LANES), kr_rot.reshape(d, LANES)], axis=1)

    uq = mla_w_uq[l].astype(BF16).reshape(MLA_Q_RANK, MLA_HEADS, MLA_NOPE + MLA_ROPE)
    half = MLA_ROPE // 2
    rope = uq[..., MLA_NOPE:]
    wq = jnp.concatenate([uq, rope[..., half:], rope[..., :half]], axis=-1)
    hw = MLA_HEADS * LANES
    wq = wq.reshape(MLA_Q_RANK, hw)
    ukv = mla_w_ukv[l].astype(BF16).reshape(MLA_KV_RANK, MLA_HEADS, MLA_NOPE + MLA_V)
    wk = _pad_heads(ukv[..., :MLA_NOPE].reshape(MLA_KV_RANK, -1), MLA_HEADS, MLA_NOPE)
    wv = _pad_heads(ukv[..., MLA_NOPE:].reshape(MLA_KV_RANK, -1), MLA_HEADS, MLA_V).T
    return (w_mla, hgrn, guv, gates), (wq, wk, wv)


def kernel(x, mem, positions, mix_norm, w_in, mla_q_norm, mla_w_uq, mla_kv_norm, mla_w_ukv,
           hg_lower_bounds, hg_head_norm, gm_ln_g, gm_ln_b, gm_w_s, gm_b_s,
           w_branch_a, w_branch_b, w_branch_c, w_mix_out,
           xa_norm, mem_norm, xa_w_q, xa_w_kv, xa_w_o,
           ffn_norm, ffn_w_in, ffn_w_out, final_norm):
    batch, seq, d = x.shape
    depth = w_in.shape[0]
    n_mem = mem.shape[1]
    t = batch * seq
    x2 = x.reshape(t, d)
    mem2 = mem.reshape(batch * n_mem, d)
    vec = lambda a: a.reshape(1, -1).astype(F32)

    tm = min(512, seq)
    inv_freq = ROPE_BASE ** (-jnp.arange(0, MLA_ROPE, 2, dtype=F32) / MLA_ROPE)
    invf_lane = jnp.zeros((LANES,), F32).at[MLA_NOPE:MLA_NOPE + MLA_ROPE].set(
        jnp.concatenate([inv_freq, inv_freq])).reshape(1, LANES)
    ct, st = _rope_tables(positions.astype(F32).reshape(t, 1), invf_lane, tm)
    merge_b = [w.astype(BF16) for w in (w_branch_a, w_branch_b, w_branch_c, w_mix_out)]
    xa_q_b, xa_kv_b, xa_o_b = (w.astype(BF16) for w in (xa_w_q, xa_w_kv, xa_w_o))
    ffn_in_b, ffn_out_b = ffn_w_in.astype(BF16), ffn_w_out.astype(BF16)

    for l in range(depth):
        in_weights, mla_weights = _layer_weights(l, w_in, mla_w_uq, mla_w_ukv)
        bias_full = jnp.repeat(gm_b_s[l].T, GM_W // GM_GROUPS, axis=1)
        gmlp_params = (vec(gm_ln_g[l]), vec(gm_ln_b[l]), gm_w_s[l], bias_full)
        z_mla, z_hqig, z_hf, oc, z_gates = _inproj(x2, vec(mix_norm[l]), in_weights, gmlp_params, tm)
        q, k, v = _mla_prep(z_mla, ct, st, vec(mla_q_norm[l]), vec(mla_kv_norm[l]), *mla_weights, tm)
        oa = _attention(q, k, v, batch, seq, min(ATTN_TQ, seq))
        ob = _hgrn(hg_lower_bounds.astype(F32), vec(hg_head_norm[l]), z_hqig, z_hf, batch, seq, l, tm)
        kv = _memkv(mem2, vec(mem_norm[l]), xa_kv_b, l)
        x2 = _merge_xattn(x2, oa, ob, oc, z_gates, merge_b, vec(xa_norm[l]), xa_q_b, kv, xa_o_b,
                          l, seq, n_mem, tm)
        x2 = _ffn(x2, vec(ffn_norm[l]), ffn_in_b, ffn_out_b, vec(final_norm), l, tm)
    return x2.reshape(batch, seq, d)
```

```python
import functools
import math

import jax
import jax.numpy as jnp
from jax import lax
from jax.experimental import pallas as pl
from jax.experimental.pallas import tpu as pltpu

F32 = jnp.float32
BF16 = jnp.bfloat16

EPS = 1e-6
LANES = 128
SUBLANES = 8
MXU_TILE = 256
LOG2E = math.log2(math.e)
VMEM_LIMIT = 56 * 1024 * 1024

MLA_HEADS = 8
MLA_Q_RANK = 384
MLA_KV_RANK = 256
MLA_NOPE = 64
MLA_ROPE = 32
MLA_V = 64
ROPE_BASE = 10000.0
HG_HEADS = 4
HG_KEY = 128
HG_CHUNK = 64
HG_SUB = 16
HG_W = HG_HEADS * HG_KEY
HG_CHUNKS_PER_TRIP = 8
GM_GROUPS = 4
GM_CHUNK = 128
GM_W = 512
X_HEADS = 4
NEG_BIG = -1e30
ATTN_HEADS_PER_STEP = 4
ATTN_PAIRS_PER_TRIP = 2
ATTN_TQ = 512
ATTN_TK = ATTN_TQ // 2

SEG_MLA = MLA_Q_RANK + MLA_KV_RANK + 2 * LANES
SEG_HQIG = 3 * HG_W
SEG_HF = HG_W
SEG_GATES = 3 * 1024


def _params(n_axes=1):
    return pltpu.CompilerParams(
        dimension_semantics=("arbitrary",) * n_axes, vmem_limit_bytes=VMEM_LIMIT)


def _rms(x, g):
    return x * lax.rsqrt(jnp.mean(x * x, axis=-1, keepdims=True) + EPS) * g


def _resident(shape):
    zeros = (0,) * len(shape)
    return pl.BlockSpec(shape, lambda *_: zeros, pipeline_mode=pl.Buffered(1))


def _layer_resident(stacked, layer):
    index = (layer, 0, 0)
    return pl.BlockSpec((None,) + stacked.shape[1:], lambda *_: index, pipeline_mode=pl.Buffered(1))


def _rope_kernel(pos_ref, invf_ref, ct_ref, st_ref):
    ang = pos_ref[...] * invf_ref[...]
    lane = lax.broadcasted_iota(jnp.int32, ang.shape, 1)
    c, s = jnp.cos(ang), jnp.sin(ang)
    in_rope = (lane >= MLA_NOPE) & (lane < MLA_NOPE + MLA_ROPE)
    first = lane < MLA_NOPE + MLA_ROPE // 2
    ct_ref[...] = jnp.where(lane < MLA_NOPE, 1.0, jnp.where(in_rope, c, 0.0))
    st_ref[...] = jnp.where(in_rope, jnp.where(first, -s, s), 0.0)


def _rope_tables(pos_f, invf_lane, tm):
    t = pos_f.shape[0]
    return pl.pallas_call(
        _rope_kernel,
        grid=(t // tm,),
        in_specs=[pl.BlockSpec((tm, 1), lambda i: (i, 0)), _resident((1, LANES))],
        out_specs=[pl.BlockSpec((tm, LANES), lambda i: (i, 0))] * 2,
        out_shape=[jax.ShapeDtypeStruct((t, LANES), F32)] * 2,
        compiler_params=_params(),
        name="rope_tables",
    )(pos_f, invf_lane)


def _inproj_kernel(x_ref, g_ref, wm_ref, wh_ref, wguv_ref, wgates_ref, ln_g_ref, ln_b_ref, ws_ref,
                   bias_ref, mla_ref, hqig_ref, hf_ref, oc_ref, gates_ref, *, nchunk):
    h = _rms(x_ref[...], g_ref[...]).astype(BF16)

    def project(w_ref, c0, c1):
        return jnp.dot(h, w_ref[:, c0:c1], preferred_element_type=F32)

    def gelu(z):
        return 0.5 * z * (1.0 + lax.erf(z * (2.0 ** -0.5)))

    for c, (out, o0) in enumerate(((hqig_ref, 0), (hf_ref, 0), (hqig_ref, HG_W), (hqig_ref, 2 * HG_W))):
        out[:, o0:o0 + HG_W] = project(wh_ref, c * HG_W, (c + 1) * HG_W).astype(out.dtype)

    u, v = gelu(project(wguv_ref, 0, GM_W)), gelu(project(wguv_ref, GM_W, 2 * GM_W))
    mu = jnp.mean(v, axis=-1, keepdims=True)
    var = jnp.mean(jnp.square(v - mu), axis=-1, keepdims=True)
    vn = ((v - mu) * lax.rsqrt(var + EPS) * ln_g_ref[...] + ln_b_ref[...]).astype(BF16)
    T = GM_CHUNK
    tri = lax.broadcasted_iota(jnp.int32, (T, T), 1) <= lax.broadcasted_iota(jnp.int32, (T, T), 0)
    for g in range(GM_GROUPS):
        gs = slice(g * LANES, (g + 1) * LANES)
        w = jnp.where(tri, ws_ref[g], 0.0).astype(BF16)
        bias = bias_ref[:, gs]
        for n in range(x_ref.shape[0] // T):
            rs = slice(n * T, (n + 1) * T)
            mixed = jnp.dot(w, vn[rs, gs], preferred_element_type=F32) + bias
            oc_ref[rs, gs] = (u[rs, gs] * mixed).astype(oc_ref.dtype)

    for out, w_ref in ((mla_ref, wm_ref), (gates_ref, wgates_ref)):
        width = out.shape[1]
        for c0 in range(0, width, nchunk):
            c1 = min(c0 + nchunk, width)
            out[:, c0:c1] = project(w_ref, c0, c1).astype(out.dtype)


def _inproj(x2, g, weights, gmlp_params, tm):
    t, d = x2.shape
    widths = (SEG_MLA, SEG_HQIG, SEG_HF, GM_W, SEG_GATES)
    dtypes = (BF16, BF16, F32, BF16, BF16)
    consts = tuple(weights) + tuple(gmlp_params)
    return pl.pallas_call(
        functools.partial(_inproj_kernel, nchunk=512),
        grid=(t // tm,),
        in_specs=[pl.BlockSpec((tm, d), lambda i: (i, 0)), _resident((1, d))]
                 + [_resident(w.shape) for w in consts],
        out_specs=[pl.BlockSpec((tm, w), lambda i: (i, 0)) for w in widths],
        out_shape=[jax.ShapeDtypeStruct((t, w), dt) for w, dt in zip(widths, dtypes)],
        compiler_params=_params(),
        name="in_proj",
    )(x2, g, *consts)


def _mla_prep_kernel(z_ref, ct_ref, st_ref, qg_ref, kvg_ref, wq_ref, wk_ref, wv_ref,
                     q_ref, k_ref, v_ref, *, scale):
    ct, st = ct_ref[...], st_ref[...]
    cq = z_ref[:, :MLA_Q_RANK].astype(F32)
    qn = _rms(cq, qg_ref[...]).astype(BF16)
    q = jnp.dot(qn, wq_ref[...], preferred_element_type=F32)
    ckv = z_ref[:, MLA_Q_RANK:MLA_Q_RANK + MLA_KV_RANK].astype(F32)
    kvn = _rms(ckv, kvg_ref[...]).astype(BF16)
    kn = jnp.dot(kvn, wk_ref[...], preferred_element_type=F32)
    vt = lax.dot_general(wv_ref[...], kvn, (((1,), (1,)), ((), ())), preferred_element_type=F32)
    row = lax.broadcasted_iota(jnp.int32, vt.shape, 0)
    vt = jnp.where(row % LANES == MLA_V, 1.0, vt).astype(BF16)
    for n in range(v_ref.shape[0]):
        v_ref[n] = vt[:, n * ATTN_TK:(n + 1) * ATTN_TK]
    o = MLA_Q_RANK + MLA_KV_RANK
    krope = z_ref[:, o:o + LANES].astype(F32) * ct + z_ref[:, o + LANES:o + 2 * LANES].astype(F32) * st
    for h in range(MLA_HEADS):
        sl = slice(h * LANES, (h + 1) * LANES)
        q_h = q[:, sl]
        swapped = pltpu.roll(q_h, LANES - MLA_ROPE, axis=1)
        q_ref[:, sl] = ((q_h * ct + swapped * st) * scale).astype(BF16)
        k_ref[:, sl] = (kn[:, sl] + krope).astype(BF16)


def _mla_prep(z_mla, ct, st, qg, kvg, wq, wk, wv, tm):
    t = z_mla.shape[0]
    hw = MLA_HEADS * LANES
    row = lambda w: pl.BlockSpec((tm, w), lambda i: (i, 0))
    return pl.pallas_call(
        functools.partial(_mla_prep_kernel, scale=float((MLA_NOPE + MLA_ROPE) ** -0.5 * math.log2(math.e))),
        grid=(t // tm,),
        in_specs=[row(SEG_MLA), row(LANES), row(LANES), _resident(qg.shape), _resident(kvg.shape),
                  _resident(wq.shape), _resident(wk.shape), _resident(wv.shape)],
        out_specs=[row(hw), row(hw), pl.BlockSpec((tm // ATTN_TK, hw, ATTN_TK), lambda i: (i, 0, 0))],
        out_shape=[jax.ShapeDtypeStruct((t, hw), BF16)] * 2
                  + [jax.ShapeDtypeStruct((t // ATTN_TK, hw, ATTN_TK), BF16)],
        compiler_params=_params(),
        name="mla_prep",
    )(z_mla, ct, st, qg, kvg, wq, wk, wv)


def _attn_kernel(q_ref, k_ref, vt_ref, o_ref, *scratch, tq):
    tk = tq // 2
    i = pl.program_id(2)
    per_head = len(scratch) // ATTN_HEADS_PER_STEP
    every, lo, hi = slice(0, tq), slice(0, tk), slice(tk, tq)
    causal = (lax.broadcasted_iota(jnp.int32, (tk, tk), 0)
              <= lax.broadcasted_iota(jnp.int32, (tk, tk), 1))

    def kv_rows(j):
        return pl.ds(pl.multiple_of(j * tk, tk), tk)

    class Head:
        def __init__(self, n):
            (s0, s1, p0, p1, al0, al1, self.m, self.acc) = scratch[n * per_head:(n + 1) * per_head]
            self.s, self.p, self.al = (s0, s1), (p0, p1), (al0, al1)
            self.lanes = slice(n * LANES, (n + 1) * LANES)

        def scores(self, j, slot, cols=every):
            self.s[slot][:, cols] = lax.dot_general(
                k_ref[kv_rows(j), self.lanes], q_ref[cols, self.lanes],
                (((1,), (1,)), ((), ())), preferred_element_type=F32)

        def accumulate(self, j, slot, cols=every):
            self.acc[:, cols] = self.al[slot][:, cols] * self.acc[:, cols] + jnp.dot(
                vt_ref[j, self.lanes, :], self.p[slot][:, cols], preferred_element_type=F32)

        def softmax(self, slot, cols=every, masked=False):
            s = self.s[slot][:, cols]
            if masked:
                s = jnp.where(causal, s, NEG_BIG)
            m = self.m[:, cols]
            m_new = jnp.maximum(m, jnp.max(s, axis=0, keepdims=True))
            self.p[slot][:, cols] = jnp.exp2(s - m_new).astype(BF16)
            self.al[slot][:, cols] = jnp.exp2(m - m_new)
            self.m[:, cols] = m_new

        def start(self):
            self.m[...] = jnp.full_like(self.m, NEG_BIG)
            self.acc[...] = jnp.zeros_like(self.acc)
            self.p[1][...] = jnp.zeros_like(self.p[1])
            self.al[1][...] = jnp.ones_like(self.al[1])
            self.scores(0, 0)

        def pair(self, p):
            a = 2 * p
            self.scores(a + 1, 1)
            self.accumulate(jnp.maximum(a - 1, 0), 1)
            self.softmax(0)
            self.scores(a + 2, 0)
            self.accumulate(a, 0)
            self.softmax(1)

        def finish(self):
            a = 2 * i
            self.scores(a + 1, 1, hi)
            self.accumulate(jnp.maximum(a - 1, 0), 1)
            self.softmax(0, lo, masked=True)
            self.softmax(0, hi)
            self.accumulate(a, 0)
            self.softmax(1, hi, masked=True)
            self.accumulate(a + 1, 1, hi)
            acc = self.acc[...]
            return (acc / acc[MLA_V:MLA_V + 1, :]).T[:, :MLA_V]

    heads = [Head(n) for n in range(ATTN_HEADS_PER_STEP)]
    for head in heads:
        head.start()

    def pairs(first, count):
        def body(t, carry):
            for head in heads:
                for n in range(count):
                    head.pair(first + count * t + n)
            return carry
        return body

    unrolled = i // ATTN_PAIRS_PER_TRIP
    lax.fori_loop(0, unrolled, pairs(0, ATTN_PAIRS_PER_TRIP), 0)
    lax.fori_loop(0, i - unrolled * ATTN_PAIRS_PER_TRIP, pairs(unrolled * ATTN_PAIRS_PER_TRIP, 1), 0)
    o_ref[...] = jnp.concatenate([head.finish() for head in heads], axis=1).astype(o_ref.dtype)


def _attention(q, k, vt, batch, seq, tq):
    t = q.shape[0]
    nq = seq // tq
    tk = tq // 2
    width = ATTN_HEADS_PER_STEP * LANES
    qspec = pl.BlockSpec((tq, width), lambda b, h, i: (b * nq + i, h))
    kspec = pl.BlockSpec((seq, width), lambda b, h, i: (b, h))
    vtspec = pl.BlockSpec((seq // tk, width, tk), lambda b, h, i: (b, h, 0))
    per_head = [pltpu.VMEM((tk, tq), F32), pltpu.VMEM((tk, tq), F32),
                pltpu.VMEM((tk, tq), BF16), pltpu.VMEM((tk, tq), BF16),
                pltpu.VMEM((1, tq), F32), pltpu.VMEM((1, tq), F32),
                pltpu.VMEM((1, tq), F32), pltpu.VMEM((LANES, tq), F32)]
    return pl.pallas_call(
        functools.partial(_attn_kernel, tq=tq),
        grid=(batch, MLA_HEADS // ATTN_HEADS_PER_STEP, nq),
        in_specs=[qspec, kspec, vtspec],
        out_specs=pl.BlockSpec((tq, ATTN_HEADS_PER_STEP * MLA_V), lambda b, h, i: (b * nq + i, h)),
        out_shape=jax.ShapeDtypeStruct((t, MLA_HEADS * MLA_V), BF16),
        scratch_shapes=per_head * ATTN_HEADS_PER_STEP,
        compiler_params=_params(3),
        name="mla_attention",
    )(q, k, vt)


def _hgrn_kernel(lbp_ref, hn_ref, qig_ref, f_ref, o_ref, st_ref, b_scr, k_scr, q_scr, v_scr, o_scr,
                 prod_scr, rsum_scr, *, layer, n_chunks):
    C, SUB, K, HALF = HG_CHUNK, HG_SUB, HG_KEY, SUBLANES

    @pl.when(pl.program_id(1) == 0)
    def _():
        st_ref[...] = jnp.zeros_like(st_ref)

    fr = f_ref[...]
    e = jnp.exp(-jnp.abs(fr))
    inv = 1.0 / (1.0 + e)
    log_sig = jnp.minimum(fr, 0.0) + jnp.log(inv)
    sig_neg = jnp.where(fr >= 0.0, e * inv, inv)
    if layer == 0:
        log_f, kk = log_sig, sig_neg
    else:
        lbp = lbp_ref[...]
        w = jnp.exp(lbp - jnp.max(lbp, axis=0, keepdims=True))
        gamma = w / jnp.sum(w, axis=0, keepdims=True)
        lb = gamma[1:2, :]
        for j in range(2, layer + 1):
            lb = lb + gamma[j:j + 1, :]
        log_lb = jnp.log(lb)
        y = jnp.log1p(-lb) + log_sig
        log_f = jnp.maximum(log_lb, y) + jnp.log1p(jnp.exp(-jnp.abs(log_lb - y)))
        kk = (1.0 - lb) * sig_neg
    hq = qig_ref[:, 0:HG_W].astype(F32)
    qq = hq * jax.nn.sigmoid(hq)
    vv = qig_ref[:, HG_W:2 * HG_W].astype(F32)
    log_f = log_f * LOG2E
    p0 = log_f.astype(BF16)
    r1 = log_f - p0.astype(F32)
    p1 = r1.astype(BF16)
    p2 = (r1 - p1.astype(F32)).astype(BF16)
    tri = (lax.broadcasted_iota(jnp.int32, (C, C), 1)
           <= lax.broadcasted_iota(jnp.int32, (C, C), 0)).astype(BF16)
    for c in range(n_chunks):
        rs = slice(c * C, (c + 1) * C)
        b_scr[c] = (jnp.dot(tri, p0[rs], preferred_element_type=F32)
                    + jnp.dot(tri, p1[rs], preferred_element_type=F32)
                    + jnp.dot(tri, p2[rs], preferred_element_type=F32))
        k_scr[c], q_scr[c], v_scr[c] = kk[rs], qq[rs], vv[rs]

    ones = jnp.ones((K, K), BF16)
    t_idx = lax.broadcasted_iota(jnp.int32, (HALF, K), 0)
    keep = [t_idx >= s for s in range(HALF)]

    def chunk(c, slot):
        heads = [slice(h * K, (h + 1) * K) for h in range(HG_HEADS)]
        blocks = range(0, C, SUB)
        bs = [b_scr[c, :, hs] for hs in heads]
        qs = [q_scr[c, :, hs] for hs in heads]
        ks = [k_scr[c, :, hs] for hs in heads]
        vs = [qig_ref[pl.ds(pl.multiple_of(c * C, C), C), HG_W + h * K:HG_W + (h + 1) * K]
              for h in range(HG_HEADS)]

        o_state = []
        for h, hs in enumerate(heads):
            state = st_ref[h]
            o_state.append(lax.dot_general(
                (qs[h] * jnp.exp2(bs[h])).astype(BF16), state.astype(BF16),
                (((1,), (1,)), ((), ())), preferred_element_type=F32))
            blast = bs[h][C - 1:C, :]
            kdec = (ks[h] * jnp.exp2(blast - bs[h])).astype(BF16)
            st_ref[h] = state * jnp.exp2(blast) + jnp.dot(
                v_scr[c, :, hs].T.astype(BF16), kdec, preferred_element_type=F32)

        a_off = {}
        for h in range(HG_HEADS):
            for r in blocks[1:]:
                bref = bs[h][r - 1:r, :]
                qd = (qs[h][r:r + SUB] * jnp.exp2(bs[h][r:r + SUB] - bref)).astype(BF16)
                kd = (ks[h][0:r] * jnp.exp2(bref - bs[h][0:r])).astype(BF16)
                a_off[h, r] = lax.dot_general(qd, kd, (((1,), (1,)), ((), ())),
                                              preferred_element_type=F32)

        owner = []
        for h, hs in enumerate(heads):
            for r in blocks:
                prods = []
                for s in range(SUB):
                    b_row, k_row = b_scr[c, r + s:r + s + 1, hs], k_scr[c, r + s:r + s + 1, hs]
                    for half in range(s // HALF, SUB // HALF):
                        t0 = r + half * HALF
                        arg = bs[h][t0:t0 + HALF] - b_row
                        if half == s // HALF:
                            arg = jnp.where(keep[s % HALF], arg, -jnp.inf)
                        prods.append(qs[h][t0:t0 + HALF] * jnp.exp2(arg) * k_row)
                        owner.append((h, t0, r + s))
                n0 = (len(owner) - len(prods)) * HALF
                prod_scr[slot, n0:n0 + len(prods) * HALF, :] = jnp.concatenate(
                    prods, axis=0).astype(BF16)
        rsum_scr[slot] = jnp.dot(prod_scr[slot], ones, preferred_element_type=F32)

        o_half = {}
        for h in range(HG_HEADS):
            for r in blocks:
                o_i = o_state[h][r:r + SUB]
                if r > 0:
                    o_i = o_i + jnp.dot(a_off[h, r].astype(BF16), vs[h][0:r],
                                        preferred_element_type=F32)
                for half in range(SUB // HALF):
                    o_half[h, r + half * HALF] = o_i[half * HALF:(half + 1) * HALF]
        for n, (h, t0, s_row) in enumerate(owner):
            o_half[h, t0] = o_half[h, t0] + (rsum_scr[slot, n * HALF:(n + 1) * HALF, :]
                                             * v_scr[c, s_row:s_row + 1, heads[h]])
        for h, hs in enumerate(heads):
            o_scr[c, :, hs] = jnp.concatenate([o_half[h, t0] for t0 in range(0, C, HALF)], axis=0)

    def chunks(t, carry):
        for slot in range(HG_CHUNKS_PER_TRIP):
            chunk(HG_CHUNKS_PER_TRIP * t + slot, slot)
        return carry

    lax.fori_loop(0, n_chunks // HG_CHUNKS_PER_TRIP, chunks, 0)

    hn = hn_ref[...]
    for c in range(n_chunks):
        rs = slice(c * C, (c + 1) * C)
        for h in range(HG_HEADS):
            hs = slice(h * K, (h + 1) * K)
            g_h = qig_ref[rs, 2 * HG_W + h * K:2 * HG_W + (h + 1) * K].astype(F32)
            o_ref[rs, hs] = (_rms(o_scr[c, :, hs], hn)
                             * (g_h * jax.nn.sigmoid(g_h))).astype(o_ref.dtype)


def _hgrn(lbp, hn, z_hqig, z_hf, batch, seq, layer, tt):
    t = z_hf.shape[0]
    nt = seq // tt
    row = lambda w: pl.BlockSpec((tt, w), lambda b, i: (b * nt + i, 0))
    tile = pltpu.VMEM((tt // HG_CHUNK, HG_CHUNK, HG_W), F32)
    halves = HG_SUB // SUBLANES
    pair_rows = HG_HEADS * (HG_CHUNK // HG_SUB) * (halves * (halves + 1) // 2 * SUBLANES) * SUBLANES
    return pl.pallas_call(
        functools.partial(_hgrn_kernel, layer=layer, n_chunks=tt // HG_CHUNK),
        grid=(batch, nt),
        in_specs=[_resident(lbp.shape), _resident(hn.shape), row(SEG_HQIG), row(SEG_HF)],
        out_specs=row(HG_W),
        out_shape=jax.ShapeDtypeStruct((t, HG_W), BF16),
        scratch_shapes=[pltpu.VMEM((HG_HEADS, HG_KEY, HG_KEY), F32), tile, tile, tile, tile, tile,
                        pltpu.VMEM((HG_CHUNKS_PER_TRIP, pair_rows, HG_KEY), BF16),
                        pltpu.VMEM((HG_CHUNKS_PER_TRIP, pair_rows, HG_KEY), F32)],
        compiler_params=_params(2),
        name="hgrn2",
    )(lbp, hn, z_hqig, z_hf)


def _memkv_kernel(m_ref, g_ref, w_ref, o_ref):
    h = _rms(m_ref[...], g_ref[...]).astype(BF16)
    o_ref[...] = jnp.dot(h, w_ref[...], preferred_element_type=F32).astype(o_ref.dtype)


def _memkv(mem2, g, w_kv, layer):
    n, d = mem2.shape
    return pl.pallas_call(
        _memkv_kernel,
        grid=(1,),
        in_specs=[_resident(mem2.shape), _resident(g.shape), _layer_resident(w_kv, layer)],
        out_specs=pl.BlockSpec((n, 2 * d), lambda i: (0, 0)),
        out_shape=jax.ShapeDtypeStruct((n, 2 * d), BF16),
        compiler_params=_params(),
        name="mem_kv",
    )(mem2, g, w_kv)


def _merge_xattn_kernel(x_ref, oa_ref, ob_ref, oc_ref, gates_ref, wa_ref, wb_ref, wc_ref, wm_ref,
                        g_ref, wq_ref, kv_ref, wo_ref, o_ref, *, scale):
    d = x_ref.shape[1]
    m = None
    for idx, (br, w) in enumerate(((oa_ref, wa_ref), (ob_ref, wb_ref), (oc_ref, wc_ref))):
        y = jnp.dot(br[...], w[...], preferred_element_type=F32)
        term = jax.nn.sigmoid(gates_ref[:, idx * d:(idx + 1) * d].astype(F32)) * y
        m = term if m is None else m + term
    x = x_ref[...] + jnp.dot(m.astype(BF16), wm_ref[...], preferred_element_type=F32)

    hd = d // X_HEADS
    h = _rms(x, g_ref[...]).astype(BF16)
    q = (jnp.dot(h, wq_ref[...], preferred_element_type=F32) * scale).astype(BF16)
    outs = []
    for hh in range(X_HEADS):
        k = kv_ref[:, hh * hd:(hh + 1) * hd]
        v = kv_ref[:, d + hh * hd:d + (hh + 1) * hd]
        s = lax.dot_general(q[:, hh * hd:(hh + 1) * hd], k, (((1,), (1,)), ((), ())),
                            preferred_element_type=F32)
        p = jnp.exp(s - jnp.max(s, axis=-1, keepdims=True))
        p = p / jnp.sum(p, axis=-1, keepdims=True)
        outs.append(jnp.dot(p.astype(BF16), v, preferred_element_type=F32).astype(BF16))
    o = jnp.concatenate(outs, axis=-1)
    o_ref[...] = x + jnp.dot(o, wo_ref[...], preferred_element_type=F32)


def _merge_xattn(x2, oa, ob, oc, gates, merge_weights, g, wq, kv, wo, layer, seq, n_mem, tm):
    t, d = x2.shape
    per_batch = seq // tm
    row = lambda w: pl.BlockSpec((tm, w), lambda i: (i, 0))
    return pl.pallas_call(
        functools.partial(_merge_xattn_kernel, scale=float((d // X_HEADS) ** -0.5)),
        grid=(t // tm,),
        in_specs=[row(d), row(oa.shape[1]), row(ob.shape[1]), row(oc.shape[1]), row(SEG_GATES)]
                 + [_layer_resident(w, layer) for w in merge_weights]
                 + [_resident(g.shape), _layer_resident(wq, layer),
                    pl.BlockSpec((n_mem, 2 * d), lambda i: (i // per_batch, 0)),
                    _layer_resident(wo, layer)],
        out_specs=row(d),
        out_shape=jax.ShapeDtypeStruct((t, d), F32),
        compiler_params=_params(),
        name="merge_cross_attn",
    )(x2, oa, ob, oc, gates, *merge_weights, g, wq, kv, wo)


def _ffn_kernel(x_ref, g_ref, wi_ref, wo_ref, fg_ref, o_ref, *, d_ff, n_split, final):
    x = x_ref[...]
    h = _rms(x, g_ref[...]).astype(BF16)
    acc = x
    tiles = d_ff // MXU_TILE
    edges = [(c * tiles // n_split) * MXU_TILE for c in range(n_split + 1)]
    for c0, c1 in zip(edges[:-1], edges[1:]):
        gt = jnp.dot(h, wi_ref[:, c0:c1], preferred_element_type=F32)
        up = jnp.dot(h, wi_ref[:, d_ff + c0:d_ff + c1], preferred_element_type=F32)
        a = (gt * jax.nn.sigmoid(gt) * up).astype(BF16)
        acc = acc + jnp.dot(a, wo_ref[c0:c1, :], preferred_element_type=F32)
    o_ref[...] = _rms(acc, fg_ref[...]) if final else acc


def _ffn(x2, g, w_in, w_out, final_g, layer, tm):
    t, d = x2.shape
    d_ff = w_out.shape[1]
    final = layer == w_in.shape[0] - 1
    row = pl.BlockSpec((tm, d), lambda i: (i, 0))
    return pl.pallas_call(
        functools.partial(_ffn_kernel, d_ff=d_ff, n_split=2, final=final),
        grid=(t // tm,),
        in_specs=[row, _resident(g.shape), _layer_resident(w_in, layer), _layer_resident(w_out, layer),
                  _resident(final_g.shape)],
        out_specs=row,
        out_shape=jax.ShapeDtypeStruct((t, d), F32),
        compiler_params=_params(),
        name="ffn",
    )(x2, g, w_in, w_out, final_g)


def _pad_heads(w, heads, width):
    r = w.shape[0]
    w = w.reshape(r, heads, width)
    return jnp.pad(w, ((0, 0), (0, 0), (0, LANES - width))).reshape(r, heads * LANES)


def _rope_slab(w_rope):
    half = MLA_ROPE // 2
    pad = ((0, 0), (0, 0), (MLA_NOPE, LANES - MLA_NOPE - MLA_ROPE))
    swapped = jnp.concatenate([w_rope[..., half:], w_rope[..., :half]], axis=-1)
    return jnp.pad(w_rope, pad), jnp.pad(swapped, pad)


def _layer_weights(l, w_in, mla_w_uq, mla_w_ukv):
    w = w_in[l]
    d = w.shape[0]
    o = [0]
    for s in (MLA_Q_RANK + MLA_KV_RANK, MLA_ROPE, 4 * HG_W, 2 * GM_W, 3 * d):
        o.append(o[-1] + s)
    latents, k_r, hgrn, guv, gates = (w[:, o[i]:o[i + 1]].astype(BF16) for i in range(5))
    kr, kr_rot = _rope_slab(k_r.reshape(d, 1, MLA_ROPE))
    w_mla = jnp.concatenate([latents, kr.reshape(d, LANES), kr_rot.reshape(d, LANES)], axis=1)

    uq = mla_w_uq[l].astype(BF16).reshape(MLA_Q_RANK, MLA_HEADS, MLA_NOPE + MLA_ROPE)
    half = MLA_ROPE // 2
    rope = uq[..., MLA_NOPE:]
    wq = jnp.concatenate([uq, rope[..., half:], rope[..., :half]], axis=-1)
    hw = MLA_HEADS * LANES
    wq = wq.reshape(MLA_Q_RANK, hw)
    ukv = mla_w_ukv[l].astype(BF16).reshape(MLA_KV_RANK, MLA_HEADS, MLA_NOPE + MLA_V)
    wk = _pad_heads(ukv[..., :MLA_NOPE].reshape(MLA_KV_RANK, -1), MLA_HEADS, MLA_NOPE)
    wv = _pad_heads(ukv[..., MLA_NOPE:].reshape(MLA_KV_RANK, -1), MLA_HEADS, MLA_V).T
    return (w_mla, hgrn, guv, gates), (wq, wk, wv)


def kernel(x, mem, positions, mix_norm, w_in, mla_q_norm, mla_w_uq, mla_kv_norm, mla_w_ukv,
           hg_lower_bounds, hg_head_norm, gm_ln_g, gm_ln_b, gm_w_s, gm_b_s,
           w_branch_a, w_branch_b, w_branch_c, w_mix_out,
           xa_norm, mem_norm, xa_w_q, xa_w_kv, xa_w_o,
           ffn_norm, ffn_w_in, ffn_w_out, final_norm):
    batch, seq, d = x.shape
    depth = w_in.shape[0]
    n_mem = mem.shape[1]
    t = batch * seq
    x2 = x.reshape(t, d)
    mem2 = mem.reshape(batch * n_mem, d)
    vec = lambda a: a.reshape(1, -1).astype(F32)

    tm = min(512, seq)
    inv_freq = ROPE_BASE ** (-jnp.arange(0, MLA_ROPE, 2, dtype=F32) / MLA_ROPE)
    invf_lane = jnp.zeros((LANES,), F32).at[MLA_NOPE:MLA_NOPE + MLA_ROPE].set(
        jnp.concatenate([inv_freq, inv_freq])).reshape(1, LANES)
    ct, st = _rope_tables(positions.astype(F32).reshape(t, 1), invf_lane, tm)
    merge_b = [w.astype(BF16) for w in (w_branch_a, w_branch_b, w_branch_c, w_mix_out)]
    xa_q_b, xa_kv_b, xa_o_b = (w.astype(BF16) for w in (xa_w_q, xa_w_kv, xa_w_o))
    ffn_in_b, ffn_out_b = ffn_w_in.astype(BF16), ffn_w_out.astype(BF16)

    for l in range(depth):
        in_weights, mla_weights = _layer_weights(l, w_in, mla_w_uq, mla_w_ukv)
        bias_full = jnp.repeat(gm_b_s[l].T, GM_W // GM_GROUPS, axis=1)
        gmlp_params = (vec(gm_ln_g[l]), vec(gm_ln_b[l]), gm_w_s[l], bias_full)
        z_mla, z_hqig, z_hf, oc, z_gates = _inproj(x2, vec(mix_norm[l]), in_weights, gmlp_params, tm)
        q, k, v = _mla_prep(z_mla, ct, st, vec(mla_q_norm[l]), vec(mla_kv_norm[l]), *mla_weights, tm)
        oa = _attention(q, k, v, batch, seq, min(ATTN_TQ, seq))
        ob = _hgrn(hg_lower_bounds.astype(F32), vec(hg_head_norm[l]), z_hqig, z_hf, batch, seq, l, tm)
        kv = _memkv(mem2, vec(mem_norm[l]), xa_kv_b, l)
        x2 = _merge_xattn(x2, oa, ob, oc, z_gates, merge_b, vec(xa_norm[l]), xa_q_b, kv, xa_o_b,
                          l, seq, n_mem, tm)
        x2 = _ffn(x2, vec(ffn_norm[l]), ffn_in_b, ffn_out_b, vec(final_norm), l, tm)
    return x2.reshape(batch, seq, d)
```

```python
import functools
import math

import jax
import jax.numpy as jnp
from jax import lax
from jax.experimental import pallas as pl
from jax.experimental.pallas import tpu as pltpu

F32 = jnp.float32
BF16 = jnp.bfloat16

EPS = 1e-6
LANES = 128
SUBLANES = 8
MXU_TILE = 256
LOG2E = math.log2(math.e)
VMEM_LIMIT = 56 * 1024 * 1024

MLA_HEADS = 8
MLA_Q_RANK = 384
MLA_KV_RANK = 256
MLA_NOPE = 64
MLA_ROPE = 32
MLA_V = 64
ROPE_BASE = 10000.0
HG_HEADS = 4
HG_KEY = 128
HG_CHUNK = 64
HG_SUB = 16
HG_W = HG_HEADS * HG_KEY
HG_CHUNKS_PER_TRIP = 8
GM_GROUPS = 4
GM_CHUNK = 128
GM_W = 512
X_HEADS = 4
NEG_BIG = -1e30
ATTN_HEADS_PER_STEP = 4
ATTN_PAIRS_PER_TRIP = 3
ATTN_TQ = 512
ATTN_TK = ATTN_TQ // 2

SEG_MLA = MLA_Q_RANK + MLA_KV_RANK + 2 * LANES
SEG_HQIG = 3 * HG_W
SEG_HF = HG_W
SEG_GATES = 3 * 1024


def _params(n_axes=1):
    return pltpu.CompilerParams(
        dimension_semantics=("arbitrary",) * n_axes, vmem_limit_bytes=VMEM_LIMIT)


def _rms(x, g):
    return x * lax.rsqrt(jnp.mean(x * x, axis=-1, keepdims=True) + EPS) * g


def _resident(shape):
    zeros = (0,) * len(shape)
    return pl.BlockSpec(shape, lambda *_: zeros, pipeline_mode=pl.Buffered(1))


def _layer_resident(stacked, layer):
    index = (layer, 0, 0)
    return pl.BlockSpec((None,) + stacked.shape[1:], lambda *_: index, pipeline_mode=pl.Buffered(1))


def _rope_kernel(pos_ref, invf_ref, ct_ref, st_ref):
    ang = pos_ref[...] * invf_ref[...]
    lane = lax.broadcasted_iota(jnp.int32, ang.shape, 1)
    c, s = jnp.cos(ang), jnp.sin(ang)
    in_rope = (lane >= MLA_NOPE) & (lane < MLA_NOPE + MLA_ROPE)
    first = lane < MLA_NOPE + MLA_ROPE // 2
    ct_ref[...] = jnp.where(lane < MLA_NOPE, 1.0, jnp.where(in_rope, c, 0.0))
    st_ref[...] = jnp.where(in_rope, jnp.where(first, -s, s), 0.0)


def _rope_tables(pos_f, invf_lane, tm):
    t = pos_f.shape[0]
    return pl.pallas_call(
        _rope_kernel,
        grid=(t // tm,),
        in_specs=[pl.BlockSpec((tm, 1), lambda i: (i, 0)), _resident((1, LANES))],
        out_specs=[pl.BlockSpec((tm, LANES), lambda i: (i, 0))] * 2,
        out_shape=[jax.ShapeDtypeStruct((t, LANES), F32)] * 2,
        compiler_params=_params(),
        name="rope_tables",
    )(pos_f, invf_lane)


def _inproj_kernel(x_ref, g_ref, wm_ref, wh_ref, wguv_ref, wgates_ref, ln_g_ref, ln_b_ref, ws_ref,
                   bias_ref, mla_ref, hqig_ref, hf_ref, oc_ref, gates_ref, *, nchunk):
    h = _rms(x_ref[...], g_ref[...]).astype(BF16)

    def project(w_ref, c0, c1):
        return jnp.dot(h, w_ref[:, c0:c1], preferred_element_type=F32)

    def gelu(z):
        return 0.5 * z * (1.0 + lax.erf(z * (2.0 ** -0.5)))

    for c, (out, o0) in enumerate(((hqig_ref, 0), (hf_ref, 0), (hqig_ref, HG_W), (hqig_ref, 2 * HG_W))):
        out[:, o0:o0 + HG_W] = project(wh_ref, c * HG_W, (c + 1) * HG_W).astype(out.dtype)

    u, v = gelu(project(wguv_ref, 0, GM_W)), gelu(project(wguv_ref, GM_W, 2 * GM_W))
    mu = jnp.mean(v, axis=-1, keepdims=True)
    var = jnp.mean(jnp.square(v - mu), axis=-1, keepdims=True)
    vn = ((v - mu) * lax.rsqrt(var + EPS) * ln_g_ref[...] + ln_b_ref[...]).astype(BF16)
    T = GM_CHUNK
    tri = lax.broadcasted_iota(jnp.int32, (T, T), 1) <= lax.broadcasted_iota(jnp.int32, (T, T), 0)
    for g in range(GM_GROUPS):
        gs = slice(g * LANES, (g + 1) * LANES)
        w = jnp.where(tri, ws_ref[g], 0.0).astype(BF16)
        bias = bias_ref[:, gs]
        for n in range(x_ref.shape[0] // T):
            rs = slice(n * T, (n + 1) * T)
            mixed = jnp.dot(w, vn[rs, gs], preferred_element_type=F32) + bias
            oc_ref[rs, gs] = (u[rs, gs] * mixed).astype(oc_ref.dtype)

    for out, w_ref in ((mla_ref, wm_ref), (gates_ref, wgates_ref)):
        width = out.shape[1]
        for c0 in range(0, width, nchunk):
            c1 = min(c0 + nchunk, width)
            out[:, c0:c1] = project(w_ref, c0, c1).astype(out.dtype)


def _inproj(x2, g, weights, gmlp_params, tm):
    t, d = x2.shape
    widths = (SEG_MLA, SEG_HQIG, SEG_HF, GM_W, SEG_GATES)
    dtypes = (BF16, BF16, F32, BF16, BF16)
    consts = tuple(weights) + tuple(gmlp_params)
    return pl.pallas_call(
        functools.partial(_inproj_kernel, nchunk=512),
        grid=(t // tm,),
        in_specs=[pl.BlockSpec((tm, d), lambda i: (i, 0)), _resident((1, d))]
                 + [_resident(w.shape) for w in consts],
        out_specs=[pl.BlockSpec((tm, w), lambda i: (i, 0)) for w in widths],
        out_shape=[jax.ShapeDtypeStruct((t, w), dt) for w, dt in zip(widths, dtypes)],
        compiler_params=_params(),
        name="in_proj",
    )(x2, g, *consts)


def _mla_prep_kernel(z_ref, ct_ref, st_ref, qg_ref, kvg_ref, wq_ref, wk_ref, wv_ref,
                     q_ref, k_ref, v_ref, *, scale):
    ct, st = ct_ref[...], st_ref[...]
    cq = z_ref[:, :MLA_Q_RANK].astype(F32)
    qn = _rms(cq, qg_ref[...]).astype(BF16)
    q = jnp.dot(qn, wq_ref[...], preferred_element_type=F32)
    ckv = z_ref[:, MLA_Q_RANK:MLA_Q_RANK + MLA_KV_RANK].astype(F32)
    kvn = _rms(ckv, kvg_ref[...]).astype(BF16)
    kn = jnp.dot(kvn, wk_ref[...], preferred_element_type=F32)
    vt = lax.dot_general(wv_ref[...], kvn, (((1,), (1,)), ((), ())), preferred_element_type=F32)
    row = lax.broadcasted_iota(jnp.int32, vt.shape, 0)
    vt = jnp.where(row % LANES == MLA_V, 1.0, vt).astype(BF16)
    for n in range(v_ref.shape[0]):
        v_ref[n] = vt[:, n * ATTN_TK:(n + 1) * ATTN_TK]
    o = MLA_Q_RANK + MLA_KV_RANK
    krope = z_ref[:, o:o + LANES].astype(F32) * ct + z_ref[:, o + LANES:o + 2 * LANES].astype(F32) * st
    for h in range(MLA_HEADS):
        sl = slice(h * LANES, (h + 1) * LANES)
        q_h = q[:, sl]
        swapped = pltpu.roll(q_h, LANES - MLA_ROPE, axis=1)
        q_ref[:, sl] = ((q_h * ct + swapped * st) * scale).astype(BF16)
        k_ref[:, sl] = (kn[:, sl] + krope).astype(BF16)


def _mla_prep(z_mla, ct, st, qg, kvg, wq, wk, wv, tm):
    t = z_mla.shape[0]
    hw = MLA_HEADS * LANES
    row = lambda w: pl.BlockSpec((tm, w), lambda i: (i, 0))
    return pl.pallas_call(
        functools.partial(_mla_prep_kernel, scale=float((MLA_NOPE + MLA_ROPE) ** -0.5 * math.log2(math.e))),
        grid=(t // tm,),
        in_specs=[row(SEG_MLA), row(LANES), row(LANES), _resident(qg.shape), _resident(kvg.shape),
                  _resident(wq.shape), _resident(wk.shape), _resident(wv.shape)],
        out_specs=[row(hw), row(hw), pl.BlockSpec((tm // ATTN_TK, hw, ATTN_TK), lambda i: (i, 0, 0))],
        out_shape=[jax.ShapeDtypeStruct((t, hw), BF16)] * 2
                  + [jax.ShapeDtypeStruct((t // ATTN_TK, hw, ATTN_TK), BF16)],
        compiler_params=_params(),
        name="mla_prep",
    )(z_mla, ct, st, qg, kvg, wq, wk, wv)


def _attn_kernel(q_ref, k_ref, vt_ref, o_ref, *scratch, tq):
    tk = tq // 2
    i = pl.program_id(2)
    per_head = len(scratch) // ATTN_HEADS_PER_STEP
    every, lo, hi = slice(0, tq), slice(0, tk), slice(tk, tq)
    causal = (lax.broadcasted_iota(jnp.int32, (tk, tk), 0)
              <= lax.broadcasted_iota(jnp.int32, (tk, tk), 1))

    def kv_rows(j):
        return pl.ds(pl.multiple_of(j * tk, tk), tk)

    class Head:
        def __init__(self, n):
            (s0, s1, p0, p1, al0, al1, self.m, self.acc) = scratch[n * per_head:(n + 1) * per_head]
            self.s, self.p, self.al = (s0, s1), (p0, p1), (al0, al1)
            self.lanes = slice(n * LANES, (n + 1) * LANES)

        def scores(self, j, slot, cols=every):
            self.s[slot][:, cols] = lax.dot_general(
                k_ref[kv_rows(j), self.lanes], q_ref[cols, self.lanes],
                (((1,), (1,)), ((), ())), preferred_element_type=F32)

        def accumulate(self, j, slot, cols=every):
            self.acc[:, cols] = self.al[slot][:, cols] * self.acc[:, cols] + jnp.dot(
                vt_ref[j, self.lanes, :], self.p[slot][:, cols], preferred_element_type=F32)

        def softmax(self, slot, cols=every, masked=False):
            s = self.s[slot][:, cols]
            if masked:
                s = jnp.where(causal, s, NEG_BIG)
            m = self.m[:, cols]
            m_new = jnp.maximum(m, jnp.max(s, axis=0, keepdims=True))
            self.p[slot][:, cols] = jnp.exp2(s - m_new).astype(BF16)
            self.al[slot][:, cols] = jnp.exp2(m - m_new)
            self.m[:, cols] = m_new

        def start(self):
            self.m[...] = jnp.full_like(self.m, NEG_BIG)
            self.acc[...] = jnp.zeros_like(self.acc)
            self.p[1][...] = jnp.zeros_like(self.p[1])
            self.al[1][...] = jnp.ones_like(self.al[1])
            self.scores(0, 0)

        def pair(self, p):
            a = 2 * p
            self.scores(a + 1, 1)
            self.accumulate(jnp.maximum(a - 1, 0), 1)
            self.softmax(0)
            self.scores(a + 2, 0)
            self.accumulate(a, 0)
            self.softmax(1)

        def finish(self):
            a = 2 * i
            self.scores(a + 1, 1, hi)
            self.accumulate(jnp.maximum(a - 1, 0), 1)
            self.softmax(0, lo, masked=True)
            self.softmax(0, hi)
            self.accumulate(a, 0)
            self.softmax(1, hi, masked=True)
            self.accumulate(a + 1, 1, hi)
            acc = self.acc[...]
            return (acc / acc[MLA_V:MLA_V + 1, :]).T[:, :MLA_V]

    heads = [Head(n) for n in range(ATTN_HEADS_PER_STEP)]
    for head in heads:
        head.start()

    def pairs(first, count):
        def body(t, carry):
            for head in heads:
                for n in range(count):
                    head.pair(first + count * t + n)
            return carry
        return body

    unrolled = i // ATTN_PAIRS_PER_TRIP
    lax.fori_loop(0, unrolled, pairs(0, ATTN_PAIRS_PER_TRIP), 0)
    lax.fori_loop(0, i - unrolled * ATTN_PAIRS_PER_TRIP, pairs(unrolled * ATTN_PAIRS_PER_TRIP, 1), 0)
    o_ref[...] = jnp.concatenate([head.finish() for head in heads], axis=1).astype(o_ref.dtype)


def _attention(q, k, vt, batch, seq, tq):
    t = q.shape[0]
    nq = seq // tq
    tk = tq // 2
    width = ATTN_HEADS_PER_STEP * LANES
    qspec = pl.BlockSpec((tq, width), lambda b, h, i: (b * nq + i, h))
    kspec = pl.BlockSpec((seq, width), lambda b, h, i: (b, h))
    vtspec = pl.BlockSpec((seq // tk, width, tk), lambda b, h, i: (b, h, 0))
    per_head = [pltpu.VMEM((tk, tq), F32), pltpu.VMEM((tk, tq), F32),
                pltpu.VMEM((tk, tq), BF16), pltpu.VMEM((tk, tq), BF16),
                pltpu.VMEM((1, tq), F32), pltpu.VMEM((1, tq), F32),
                pltpu.VMEM((1, tq), F32), pltpu.VMEM((LANES, tq), F32)]
    return pl.pallas_call(
        functools.partial(_attn_kernel, tq=tq),
        grid=(batch, MLA_HEADS // ATTN_HEADS_PER_STEP, nq),
        in_specs=[qspec, kspec, vtspec],
        out_specs=pl.BlockSpec((tq, ATTN_HEADS_PER_STEP * MLA_V), lambda b, h, i: (b * nq + i, h)),
        out_shape=jax.ShapeDtypeStruct((t, MLA_HEADS * MLA_V), BF16),
        scratch_shapes=per_head * ATTN_HEADS_PER_STEP,
        compiler_params=_params(3),
        name="mla_attention",
    )(q, k, vt)


def _hgrn_kernel(lbp_ref, hn_ref, qig_ref, f_ref, o_ref, st_ref, b_scr, k_scr, q_scr, v_scr, o_scr,
                 prod_scr, rsum_scr, *, layer, n_chunks):
    C, SUB, K, HALF = HG_CHUNK, HG_SUB, HG_KEY, SUBLANES

    @pl.when(pl.program_id(1) == 0)
    def _():
        st_ref[...] = jnp.zeros_like(st_ref)

    fr = f_ref[...]
    e = jnp.exp(-jnp.abs(fr))
    inv = 1.0 / (1.0 + e)
    log_sig = jnp.minimum(fr, 0.0) + jnp.log(inv)
    sig_neg = jnp.where(fr >= 0.0, e * inv, inv)
    if layer == 0:
        log_f, kk = log_sig, sig_neg
    else:
        lbp = lbp_ref[...]
        w = jnp.exp(lbp - jnp.max(lbp, axis=0, keepdims=True))
        gamma = w / jnp.sum(w, axis=0, keepdims=True)
        lb = gamma[1:2, :]
        for j in range(2, layer + 1):
            lb = lb + gamma[j:j + 1, :]
        log_lb = jnp.log(lb)
        y = jnp.log1p(-lb) + log_sig
        log_f = jnp.maximum(log_lb, y) + jnp.log1p(jnp.exp(-jnp.abs(log_lb - y)))
        kk = (1.0 - lb) * sig_neg
    hq = qig_ref[:, 0:HG_W].astype(F32)
    qq = hq * jax.nn.sigmoid(hq)
    vv = qig_ref[:, HG_W:2 * HG_W].astype(F32)
    log_f = log_f * LOG2E
    p0 = log_f.astype(BF16)
    r1 = log_f - p0.astype(F32)
    p1 = r1.astype(BF16)
    p2 = (r1 - p1.astype(F32)).astype(BF16)
    tri = (lax.broadcasted_iota(jnp.int32, (C, C), 1)
           <= lax.broadcasted_iota(jnp.int32, (C, C), 0)).astype(BF16)
    for c in range(n_chunks):
        rs = slice(c * C, (c + 1) * C)
        b_scr[c] = (jnp.dot(tri, p0[rs], preferred_element_type=F32)
                    + jnp.dot(tri, p1[rs], preferred_element_type=F32)
                    + jnp.dot(tri, p2[rs], preferred_element_type=F32))
        k_scr[c], q_scr[c], v_scr[c] = kk[rs], qq[rs], vv[rs]

    ones = jnp.ones((K, K), BF16)
    t_idx = lax.broadcasted_iota(jnp.int32, (HALF, K), 0)
    keep = [t_idx >= s for s in range(HALF)]

    def chunk(c, slot):
        heads = [slice(h * K, (h + 1) * K) for h in range(HG_HEADS)]
        blocks = range(0, C, SUB)
        bs = [b_scr[c, :, hs] for hs in heads]
        qs = [q_scr[c, :, hs] for hs in heads]
        ks = [k_scr[c, :, hs] for hs in heads]
        vs = [qig_ref[pl.ds(pl.multiple_of(c * C, C), C), HG_W + h * K:HG_W + (h + 1) * K]
              for h in range(HG_HEADS)]

        o_state = []
        for h, hs in enumerate(heads):
            state = st_ref[h]
            o_state.append(lax.dot_general(
                (qs[h] * jnp.exp2(bs[h])).astype(BF16), state.astype(BF16),
                (((1,), (1,)), ((), ())), preferred_element_type=F32))
            blast = bs[h][C - 1:C, :]
            kdec = (ks[h] * jnp.exp2(blast - bs[h])).astype(BF16)
            st_ref[h] = state * jnp.exp2(blast) + jnp.dot(
                v_scr[c, :, hs].T.astype(BF16), kdec, preferred_element_type=F32)

        a_off = {}
        for h in range(HG_HEADS):
            for r in blocks[1:]:
                bref = bs[h][r - 1:r, :]
                qd = (qs[h][r:r + SUB] * jnp.exp2(bs[h][r:r + SUB] - bref)).astype(BF16)
                kd = (ks[h][0:r] * jnp.exp2(bref - bs[h][0:r])).astype(BF16)
                a_off[h, r] = lax.dot_general(qd, kd, (((1,), (1,)), ((), ())),
                                              preferred_element_type=F32)

        owner = []
        for h, hs in enumerate(heads):
            for r in blocks:
                prods = []
                for s in range(SUB):
                    b_row, k_row = b_scr[c, r + s:r + s + 1, hs], k_scr[c, r + s:r + s + 1, hs]
                    for half in range(s // HALF, SUB // HALF):
                        t0 = r + half * HALF
                        arg = bs[h][t0:t0 + HALF] - b_row
                        if half == s // HALF:
                            arg = jnp.where(keep[s % HALF], arg, -jnp.inf)
                        prods.append(qs[h][t0:t0 + HALF] * jnp.exp2(arg) * k_row)
                        owner.append((h, t0, r + s))
                n0 = (len(owner) - len(prods)) * HALF
                prod_scr[slot, n0:n0 + len(prods) * HALF, :] = jnp.concatenate(
                    prods, axis=0).astype(BF16)
        rsum_scr[slot] = jnp.dot(prod_scr[slot], ones, preferred_element_type=F32)

        o_half = {}
        for h in range(HG_HEADS):
            for r in blocks:
                o_i = o_state[h][r:r + SUB]
                if r > 0:
                    o_i = o_i + jnp.dot(a_off[h, r].astype(BF16), vs[h][0:r],
                                        preferred_element_type=F32)
                for half in range(SUB // HALF):
                    o_half[h, r + half * HALF] = o_i[half * HALF:(half + 1) * HALF]
        for n, (h, t0, s_row) in enumerate(owner):
            o_half[h, t0] = o_half[h, t0] + (rsum_scr[slot, n * HALF:(n + 1) * HALF, :]
                                             * v_scr[c, s_row:s_row + 1, heads[h]])
        for h, hs in enumerate(heads):
            o_scr[c, :, hs] = jnp.concatenate([o_half[h, t0] for t0 in range(0, C, HALF)], axis=0)

    def chunks(t, carry):
        for slot in range(HG_CHUNKS_PER_TRIP):
            chunk(HG_CHUNKS_PER_TRIP * t + slot, slot)
        return carry

    lax.fori_loop(0, n_chunks // HG_CHUNKS_PER_TRIP, chunks, 0)

    hn = hn_ref[...]
    for c in range(n_chunks):
        rs = slice(c * C, (c + 1) * C)
        for h in range(HG_HEADS):
            hs = slice(h * K, (h + 1) * K)
            g_h = qig_ref[rs, 2 * HG_W + h * K:2 * HG_W + (h + 1) * K].astype(F32)
            o_ref[rs, hs] = (_rms(o_scr[c, :, hs], hn)
                             * (g_h * jax.nn.sigmoid(g_h))).astype(o_ref.dtype)


def _hgrn(lbp, hn, z_hqig, z_hf, batch, seq, layer, tt):
    t = z_hf.shape[0]
    nt = seq // tt
    row = lambda w: pl.BlockSpec((tt, w), lambda b, i: (b * nt + i, 0))
    tile = pltpu.VMEM((tt // HG_CHUNK, HG_CHUNK, HG_W), F32)
    halves = HG_SUB // SUBLANES
    pair_rows = HG_HEADS * (HG_CHUNK // HG_SUB) * (halves * (halves + 1) // 2 * SUBLANES) * SUBLANES
    return pl.pallas_call(
        functools.partial(_hgrn_kernel, layer=layer, n_chunks=tt // HG_CHUNK),
        grid=(batch, nt),
        in_specs=[_resident(lbp.shape), _resident(hn.shape), row(SEG_HQIG), row(SEG_HF)],
        out_specs=row(HG_W),
        out_shape=jax.ShapeDtypeStruct((t, HG_W), BF16),
        scratch_shapes=[pltpu.VMEM((HG_HEADS, HG_KEY, HG_KEY), F32), tile, tile, tile, tile, tile,
                        pltpu.VMEM((HG_CHUNKS_PER_TRIP, pair_rows, HG_KEY), BF16),
                        pltpu.VMEM((HG_CHUNKS_PER_TRIP, pair_rows, HG_KEY), F32)],
        compiler_params=_params(2),
        name="hgrn2",
    )(lbp, hn, z_hqig, z_hf)


def _memkv_kernel(m_ref, g_ref, w_ref, o_ref):
    h = _rms(m_ref[...], g_ref[...]).astype(BF16)
    o_ref[...] = jnp.dot(h, w_ref[...], preferred_element_type=F32).astype(o_ref.dtype)


def _memkv(mem2, g, w_kv, layer):
    n, d = mem2.shape
    return pl.pallas_call(
        _memkv_kernel,
        grid=(1,),
        in_specs=[_resident(mem2.shape), _resident(g.shape), _layer_resident(w_kv, layer)],
        out_specs=pl.BlockSpec((n, 2 * d), lambda i: (0, 0)),
        out_shape=jax.ShapeDtypeStruct((n, 2 * d), BF16),
        compiler_params=_params(),
        name="mem_kv",
    )(mem2, g, w_kv)


def _merge_xattn_kernel(x_ref, oa_ref, ob_ref, oc_ref, gates_ref, wa_ref, wb_ref, wc_ref, wm_ref,
                        g_ref, wq_ref, kv_ref, wo_ref, o_ref, *, scale):
    d = x_ref.shape[1]
    m = None
    for idx, (br, w) in enumerate(((oa_ref, wa_ref), (ob_ref, wb_ref), (oc_ref, wc_ref))):
        y = jnp.dot(br[...], w[...], preferred_element_type=F32)
        term = jax.nn.sigmoid(gates_ref[:, idx * d:(idx + 1) * d].astype(F32)) * y
        m = term if m is None else m + term
    x = x_ref[...] + jnp.dot(m.astype(BF16), wm_ref[...], preferred_element_type=F32)

    hd = d // X_HEADS
    h = _rms(x, g_ref[...]).astype(BF16)
    q = (jnp.dot(h, wq_ref[...], preferred_element_type=F32) * scale).astype(BF16)
    outs = []
    for hh in range(X_HEADS):
        k = kv_ref[:, hh * hd:(hh + 1) * hd]
        v = kv_ref[:, d + hh * hd:d + (hh + 1) * hd]
        s = lax.dot_general(q[:, hh * hd:(hh + 1) * hd], k, (((1,), (1,)), ((), ())),
                            preferred_element_type=F32)
        p = jnp.exp(s - jnp.max(s, axis=-1, keepdims=True))
        p = p / jnp.sum(p, axis=-1, keepdims=True)
        outs.append(jnp.dot(p.astype(BF16), v, preferred_element_type=F32).astype(BF16))
    o = jnp.concatenate(outs, axis=-1)
    o_ref[...] = x + jnp.dot(o, wo_ref[...], preferred_element_type=F32)


def _merge_xattn(x2, oa, ob, oc, gates, merge_weights, g, wq, kv, wo, layer, seq, n_mem, tm):
    t, d = x2.shape
    per_batch = seq // tm
    row = lambda w: pl.BlockSpec((tm, w), lambda i: (i, 0))
    return pl.pallas_call(
        functools.partial(_merge_xattn_kernel, scale=float((d // X_HEADS) ** -0.5)),
        grid=(t // tm,),
        in_specs=[row(d), row(oa.shape[1]), row(ob.shape[1]), row(oc.shape[1]), row(SEG_GATES)]
                 + [_layer_resident(w, layer) for w in merge_weights]
                 + [_resident(g.shape), _layer_resident(wq, layer),
                    pl.BlockSpec((n_mem, 2 * d), lambda i: (i // per_batch, 0)),
                    _layer_resident(wo, layer)],
        out_specs=row(d),
        out_shape=jax.ShapeDtypeStruct((t, d), F32),
        compiler_params=_params(),
        name="merge_cross_attn",
    )(x2, oa, ob, oc, gates, *merge_weights, g, wq, kv, wo)


def _ffn_kernel(x_ref, g_ref, wi_ref, wo_ref, fg_ref, o_ref, *, d_ff, n_split, final):
    x = x_ref[...]
    h = _rms(x, g_ref[...]).astype(BF16)
    acc = x
    tiles = d_ff // MXU_TILE
    edges = [(c * tiles // n_split) * MXU_TILE for c in range(n_split + 1)]
    for c0, c1 in zip(edges[:-1], edges[1:]):
        gt = jnp.dot(h, wi_ref[:, c0:c1], preferred_element_type=F32)
        up = jnp.dot(h, wi_ref[:, d_ff + c0:d_ff + c1], preferred_element_type=F32)
        a = (gt * jax.nn.sigmoid(gt) * up).astype(BF16)
        acc = acc + jnp.dot(a, wo_ref[c0:c1, :], preferred_element_type=F32)
    o_ref[...] = _rms(acc, fg_ref[...]) if final else acc


def _ffn(x2, g, w_in, w_out, final_g, layer, tm):
    t, d = x2.shape
    d_ff = w_out.shape[1]
    final = layer == w_in.shape[0] - 1
    row = pl.BlockSpec((tm, d), lambda i: (i, 0))
    return pl.pallas_call(
        functools.partial(_ffn_kernel, d_ff=d_ff, n_split=2, final=final),
        grid=(t // tm,),
        in_specs=[row, _resident(g.shape), _layer_resident(w_in, layer), _layer_resident(w_out, layer),
                  _resident(final_g.shape)],
        out_specs=row,
        out_shape=jax.ShapeDtypeStruct((t, d), F32),
        compiler_params=_params(),
        name="ffn",
    )(x2, g, w_in, w_out, final_g)


def _pad_heads(w, heads, width):
    r = w.shape[0]
    w = w.reshape(r, heads, width)
    return jnp.pad(w, ((0, 0), (0, 0), (0, LANES - width))).reshape(r, heads * LANES)


def _rope_slab(w_rope):
    half = MLA_ROPE // 2
    pad = ((0, 0), (0, 0), (MLA_NOPE, LANES - MLA_NOPE - MLA_ROPE))
    swapped = jnp.concatenate([w_rope[..., half:], w_rope[..., :half]], axis=-1)
    return jnp.pad(w_rope, pad), jnp.pad(swapped, pad)


def _layer_weights(l, w_in, mla_w_uq, mla_w_ukv):
    w = w_in[l]
    d = w.shape[0]
    o = [0]
    for s in (MLA_Q_RANK + MLA_KV_RANK, MLA_ROPE, 4 * HG_W, 2 * GM_W, 3 * d):
        o.append(o[-1] + s)
    latents, k_r, hgrn, guv, gates = (w[:, o[i]:o[i + 1]].astype(BF16) for i in range(5))
    kr, kr_rot = _rope_slab(k_r.reshape(d, 1, MLA_ROPE))
    w_mla = jnp.concatenate([latents, kr.reshape(d, LANES), kr_rot.reshape(d, LANES)], axis=1)

    uq = mla_w_uq[l].astype(BF16).reshape(MLA_Q_RANK, MLA_HEADS, MLA_NOPE + MLA_ROPE)
    half = MLA_ROPE // 2
    rope = uq[..., MLA_NOPE:]
    wq = jnp.concatenate([uq, rope[..., half:], rope[..., :half]], axis=-1)
    hw = MLA_HEADS * LANES
    wq = wq.reshape(MLA_Q_RANK, hw)
    ukv = mla_w_ukv[l].astype(BF16).reshape(MLA_KV_RANK, MLA_HEADS, MLA_NOPE + MLA_V)
    wk = _pad_heads(ukv[..., :MLA_NOPE].reshape(MLA_KV_RANK, -1), MLA_HEADS, MLA_NOPE)
    wv = _pad_heads(ukv[..., MLA_NOPE:].reshape(MLA_KV_RANK, -1), MLA_HEADS, MLA_V).T
    return (w_mla, hgrn, guv, gates), (wq, wk, wv)


def kernel(x, mem, positions, mix_norm, w_in, mla_q_norm, mla_w_uq, mla_kv_norm, mla_w_ukv,
           hg_lower_bounds, hg_head_norm, gm_ln_g, gm_ln_b, gm_w_s, gm_b_s,
           w_branch_a, w_branch_b, w_branch_c, w_mix_out,
           xa_norm, mem_norm, xa_w_q, xa_w_kv, xa_w_o,
           ffn_norm, ffn_w_in, ffn_w_out, final_norm):
    batch, seq, d = x.shape
    depth = w_in.shape[0]
    n_mem = mem.shape[1]
    t = batch * seq
    x2 = x.reshape(t, d)
    mem2 = mem.reshape(batch * n_mem, d)
    vec = lambda a: a.reshape(1, -1).astype(F32)

    tm = min(512, seq)
    inv_freq = ROPE_BASE ** (-jnp.arange(0, MLA_ROPE, 2, dtype=F32) / MLA_ROPE)
    invf_lane = jnp.zeros((LANES,), F32).at[MLA_NOPE:MLA_NOPE + MLA_ROPE].set(
        jnp.concatenate([inv_freq, inv_freq])).reshape(1, LANES)
    ct, st = _rope_tables(positions.astype(F32).reshape(t, 1), invf_lane, tm)
    merge_b = [w.astype(BF16) for w in (w_branch_a, w_branch_b, w_branch_c, w_mix_out)]
    xa_q_b, xa_kv_b, xa_o_b = (w.astype(BF16) for w in (xa_w_q, xa_w_kv, xa_w_o))
    ffn_in_b, ffn_out_b = ffn_w_in.astype(BF16), ffn_w_out.astype(BF16)

    for l in range(depth):
        in_weights, mla_weights = _layer_weights(l, w_in, mla_w_uq, mla_w_ukv)
        bias_full = jnp.repeat(gm_b_s[l].T, GM_W // GM_GROUPS, axis=1)
        gmlp_params = (vec(gm_ln_g[l]), vec(gm_ln_b[l]), gm_w_s[l], bias_full)
        z_mla, z_hqig, z_hf, oc, z_gates = _inproj(x2, vec(mix_norm[l]), in_weights, gmlp_params, tm)
        q, k, v = _mla_prep(z_mla, ct, st, vec(mla_q_norm[l]), vec(mla_kv_norm[l]), *mla_weights, tm)
        oa = _attention(q, k, v, batch, seq, min(ATTN_TQ, seq))
        ob = _hgrn(hg_lower_bounds.astype(F32), vec(hg_head_norm[l]), z_hqig, z_hf, batch, seq, l, tm)
        kv = _memkv(mem2, vec(mem_norm[l]), xa_kv_b, l)
        x2 = _merge_xattn(x2, oa, ob, oc, z_gates, merge_b, vec(xa_norm[l]), xa_q_b, kv, xa_o_b,
                          l, seq, n_mem, tm)
        x2 = _ffn(x2, vec(ffn_norm[l]), ffn_in_b, ffn_out_b, vec(final_norm), l, tm)
    return x2.reshape(batch, seq, d)
```
